```python
import math
import jax, jax.numpy as jnp
from jax import lax
import numpy as np

D_MODEL = 1024
BATCH = 2
SEQ = 16384
DEPTH = 2

N_MEM = 256
D_MIX = D_MODEL
HG_HEADS = 4
HG_WIDTH = D_MIX // 4
HG_DK = HG_WIDTH // HG_HEADS
HG_DV = HG_WIDTH // HG_HEADS
HG_CHUNK = 64
RG_WIDTH = D_MIX // 4
RG_BLOCKS = 4
RG_BLOCK = RG_WIDTH // RG_BLOCKS
RG_CONV = 4
RG_C = 8.0
DA_HEADS = 4
DA_WIDTH = D_MIX // 2
DA_V_DIM = DA_WIDTH // DA_HEADS
DA_QK_DIM = DA_V_DIM // 2
Q_BLOCK = 128
ROPE_THETA = 10000.0
CA_HEADS = 4
CA_HEAD_DIM = D_MODEL // CA_HEADS
D_FF = 2816
FFN_CONV = 3
LN_EPS = 1e-5
RMS_EPS = 1e-6
DEEPNORM_ALPHA = (2 * DEPTH) ** 0.25
DEEPNORM_BETA = (8 * DEPTH) ** -0.25
IN_SIZES = (HG_WIDTH, HG_WIDTH, HG_WIDTH, HG_WIDTH, HG_WIDTH,
            RG_WIDTH, RG_WIDTH,
            DA_WIDTH, DA_WIDTH, DA_WIDTH)
D_IN = sum(IN_SIZES)

kernel_name = "hybrid_hgrn2_rglru_diffattn_encoder"

F32 = jnp.float32


def _layer_norm(x, g, b):
    xf = x.astype(F32)
    mu = jnp.mean(xf, -1, keepdims=True)
    var = jnp.mean(jnp.square(xf - mu), -1, keepdims=True)
    return ((xf - mu) * lax.rsqrt(var + LN_EPS) * g.astype(F32) + b.astype(F32)).astype(x.dtype)


def _rms_norm(x, g):
    xf = x.astype(F32)
    return (xf * lax.rsqrt(jnp.mean(jnp.square(xf), -1, keepdims=True) + RMS_EPS) * g.astype(F32)).astype(x.dtype)


def _depthwise_conv(x, w, b, pad_left, pad_right):
    out = lax.conv_general_dilated(x, w[:, None, :].astype(x.dtype), window_strides=(1,),
                                   padding=[(pad_left, pad_right)],
                                   dimension_numbers=("NWC", "WIO", "NWC"),
                                   feature_group_count=x.shape[-1])
    return out + b.astype(x.dtype)


def _rotary(t, cos, sin):
    c = cos[:, :, None, None, :]
    s = sin[:, :, None, None, :]
    t1, t2 = jnp.split(t.astype(F32), 2, axis=-1)
    return jnp.concatenate([t1 * c - t2 * s, t2 * c + t1 * s], axis=-1).astype(t.dtype)


def _gated_linear_recurrence(q, k, v, logf):
    B, H, S, dk = q.shape
    dv = v.shape[-1]
    nc = S // HG_CHUNK

    def to_chunks(t):
        return jnp.moveaxis(t.reshape(B, H, nc, HG_CHUNK, t.shape[-1]), 2, 0)

    tri = jnp.tril(jnp.ones((HG_CHUNK, HG_CHUNK), dtype=bool))[:, :, None]

    def step(state, inp):
        qc, kc, vc, gc = inp
        b = jnp.cumsum(gc, axis=-2)
        o_inter = jnp.einsum("bhtd,bhde->bhte", qc * jnp.exp(b), state)
        diff = b[..., :, None, :] - b[..., None, :, :]
        decay = jnp.where(tri, jnp.exp(jnp.where(tri, diff, 0.0)), 0.0)
        att = jnp.einsum("bhtd,bhtsd,bhsd->bhts", qc, decay, kc)
        o_intra = jnp.einsum("bhts,bhse->bhte", att, vc)
        b_last = b[..., -1:, :]
        new_state = (jnp.exp(b_last)[:, :, 0, :, None] * state
                     + jnp.einsum("bhsd,bhse->bhde", kc * jnp.exp(b_last - b), vc))
        return new_state, o_inter + o_intra

    state0 = jnp.zeros((B, H, dk, dv), F32)
    _, o = lax.scan(step, state0, (to_chunks(q), to_chunks(k), to_chunks(v), to_chunks(logf)))
    return jnp.moveaxis(o, 0, 2).reshape(B, H, S, dv)


def _hgrn2_mixer(q, f_fwd, f_bwd, i, g, lb, norm_g):
    B, S, _ = q.shape

    def heads(t):
        return t.astype(F32).reshape(B, S, HG_HEADS, -1).transpose(0, 2, 1, 3)

    qh = heads(jax.nn.silu(q.astype(F32)))
    vh = heads(i)
    o = jnp.zeros((B, HG_HEADS, S, HG_DV), F32)
    for d, (fpre, rev) in enumerate(((f_fwd, False), (f_bwd, True))):
        lbd = lb[d].astype(F32)
        z = fpre.astype(F32)
        f = lbd + (1.0 - lbd) * jax.nn.sigmoid(z)
        logf = jnp.log(f)
        kh = heads((1.0 - lbd) * jax.nn.sigmoid(-z))
        gh = heads(logf)
        if rev:
            flip = lambda t: jnp.flip(t, axis=2)
            o = o + flip(_gated_linear_recurrence(flip(qh), flip(kh), flip(vh), flip(gh)))
        else:
            o = o + _gated_linear_recurrence(qh, kh, vh, gh)
    o = o.transpose(0, 2, 1, 3)
    o = _rms_norm(o, norm_g.reshape(HG_HEADS, HG_DV)).reshape(B, S, HG_WIDTH)
    return (o * jax.nn.silu(g.astype(F32))).astype(q.dtype)


def _linrec_combine(e1, e2):
    a1, b1 = e1
    a2, b2 = e2
    return a1 * a2, a2 * b1 + b2


def _rglru_mixer(xb, yb, conv_w, conv_b, wa, ba, wx, bx, lam):
    xc = _depthwise_conv(xb, conv_w, conv_b, RG_CONV // 2, RG_CONV - 1 - RG_CONV // 2).astype(F32)
    B, S, _ = xc.shape
    blocks = xc.reshape(B, S, RG_BLOCKS, RG_BLOCK)
    h = jnp.zeros_like(xc)
    for d in range(2):
        r = jax.nn.sigmoid(jnp.einsum("bsni,nij->bsnj", blocks, wa[d].astype(F32)).reshape(B, S, RG_WIDTH)
                           + ba[d].astype(F32))
        ig = jax.nn.sigmoid(jnp.einsum("bsni,nij->bsnj", blocks, wx[d].astype(F32)).reshape(B, S, RG_WIDTH)
                            + bx[d].astype(F32))
        log_a = -RG_C * r * jax.nn.softplus(-lam[d].astype(F32))
        a = jnp.exp(log_a)
        u = jnp.sqrt(-jnp.expm1(2.0 * log_a)) * (ig * xc)
        _, hd = lax.associative_scan(_linrec_combine, (a, u), reverse=(d == 1), axis=1)
        h = h + hd
    return (h * jax.nn.gelu(yb.astype(F32))).astype(xb.dtype)


def _diff_attention(q, k, v, cos, sin, lam_params, subln_g, layer):
    B, S, _ = q.shape
    q = _rotary(q.reshape(B, S, DA_HEADS, 2, DA_QK_DIM), cos, sin)
    k = _rotary(k.reshape(B, S, DA_HEADS, 2, DA_QK_DIM), cos, sin)
    v = v.reshape(B, S, DA_HEADS, DA_V_DIM)
    lam_init = 0.8 - 0.6 * math.exp(-0.3 * layer)
    lp = lam_params.astype(F32)
    lam = jnp.exp(jnp.sum(lp[0] * lp[1])) - jnp.exp(jnp.sum(lp[2] * lp[3])) + lam_init
    nb = S // Q_BLOCK
    qb = q.reshape(B, nb, Q_BLOCK, DA_HEADS, 2, DA_QK_DIM).transpose(1, 0, 3, 4, 2, 5)
    kt = k.transpose(0, 2, 3, 1, 4)
    vt = v.transpose(0, 2, 1, 3)
    scale = DA_QK_DIM ** -0.5

    def attend(qblk):
        s = jnp.einsum("bhcqd,bhckd->bhcqk", qblk, kt).astype(F32) * scale
        p = jax.nn.softmax(s, axis=-1)
        w = p[:, :, 0] - lam * p[:, :, 1]
        return jnp.einsum("bhqk,bhkv->bhqv", w.astype(vt.dtype), vt)

    o = lax.map(attend, qb)
    o = o.transpose(1, 0, 3, 2, 4).reshape(B, S, DA_HEADS, DA_V_DIM)
    o = _rms_norm(o, subln_g) * (1.0 - lam_init)
    return o.reshape(B, S, DA_WIDTH).astype(q.dtype)


def _memory_cross_attention(x, mem, wq, wk, wv, wo):
    B, S, _ = x.shape
    M = mem.shape[1]
    q = (x @ wq).reshape(B, S, CA_HEADS, CA_HEAD_DIM)
    k = (mem @ wk).reshape(B, M, CA_HEADS, CA_HEAD_DIM)
    v = (mem @ wv).reshape(B, M, CA_HEADS, CA_HEAD_DIM)
    s = jnp.einsum("bshd,bmhd->bhsm", q, k).astype(F32) * (CA_HEAD_DIM ** -0.5)
    p = jax.nn.softmax(s, axis=-1)
    o = jnp.einsum("bhsm,bmhd->bshd", p.astype(v.dtype), v).reshape(B, S, D_MODEL)
    return o @ wo


def _conv_glu_ffn(x, w_up, conv_w, conv_b, w_down):
    gate, val = jnp.split(x @ w_up, 2, axis=-1)
    gate = _depthwise_conv(gate, conv_w, conv_b, FFN_CONV // 2, FFN_CONV // 2)
    return (jax.nn.gelu(gate) * val) @ w_down


def setup_inputs(seed: int = 0) -> dict:
    key = jax.random.key(seed)
    ks = iter(jax.random.split(key, 40))
    L = DEPTH
    nrm = lambda shape, scale: jax.random.normal(next(ks), shape, F32) * scale
    gain = lambda shape: 1.0 + 0.02 * jax.random.normal(next(ks), shape, F32)
    small = lambda shape: 0.01 * jax.random.normal(next(ks), shape, F32)

    x = jax.random.normal(next(ks), (BATCH, SEQ, D_MODEL), F32)
    mem = jax.random.normal(next(ks), (BATCH, N_MEM, D_MODEL), F32)
    positions = jnp.broadcast_to(jnp.arange(SEQ, dtype=jnp.int32), (BATCH, SEQ))

    col_scale = jnp.concatenate([jnp.ones((D_IN - DA_WIDTH,), F32), jnp.full((DA_WIDTH,), DEEPNORM_BETA, F32)])
    w_in = nrm((L, D_MODEL, D_IN), D_MODEL ** -0.5) * col_scale

    hg_lower_bounds = nrm((L, 2, HG_WIDTH), 0.1)
    hg_norm_g = gain((L, HG_WIDTH))

    rg_conv_w = nrm((L, RG_CONV, RG_WIDTH), RG_CONV ** -0.5)
    rg_conv_b = small((L, RG_WIDTH))
    rg_wa = nrm((L, 2, RG_BLOCKS, RG_BLOCK, RG_BLOCK), RG_BLOCK ** -0.5)
    rg_ba = small((L, 2, RG_WIDTH))
    rg_wx = nrm((L, 2, RG_BLOCKS, RG_BLOCK, RG_BLOCK), RG_BLOCK ** -0.5)
    rg_bx = small((L, 2, RG_WIDTH))
    u = jax.random.uniform(next(ks), (L, 2, RG_WIDTH), F32, minval=0.9, maxval=0.999)
    a0 = u ** (1.0 / RG_C)
    rg_lambda = jnp.log(a0) - jnp.log1p(-a0)

    da_lambda = nrm((L, 4, DA_QK_DIM), 0.1)
    da_subln_g = gain((L, DA_V_DIM))

    w_out = nrm((L, D_MIX, D_MODEL), D_MIX ** -0.5 * DEEPNORM_BETA)
    ln1_g = gain((L, D_MODEL)); ln1_b = small((L, D_MODEL))

    ca_wq = nrm((L, D_MODEL, D_MODEL), D_MODEL ** -0.5)
    ca_wk = nrm((L, D_MODEL, D_MODEL), D_MODEL ** -0.5)
    ca_wv = nrm((L, D_MODEL, D_MODEL), D_MODEL ** -0.5 * DEEPNORM_BETA)
    ca_wo = nrm((L, D_MODEL, D_MODEL), D_MODEL ** -0.5 * DEEPNORM_BETA)
    ln2_g = gain((L, D_MODEL)); ln2_b = small((L, D_MODEL))

    ffn_w_up = nrm((L, D_MODEL, 2 * D_FF), D_MODEL ** -0.5 * DEEPNORM_BETA)
    ffn_conv_w = nrm((L, FFN_CONV, D_FF), FFN_CONV ** -0.5)
    ffn_conv_b = small((L, D_FF))
    ffn_w_down = nrm((L, D_FF, D_MODEL), D_FF ** -0.5 * DEEPNORM_BETA)
    ln3_g = gain((L, D_MODEL)); ln3_b = small((L, D_MODEL))

    return {"x": x, "mem": mem, "positions": positions, "w_in": w_in,
            "hg_lower_bounds": hg_lower_bounds, "hg_norm_g": hg_norm_g,
            "rg_conv_w": rg_conv_w, "rg_conv_b": rg_conv_b, "rg_wa": rg_wa, "rg_ba": rg_ba,
            "rg_wx": rg_wx, "rg_bx": rg_bx, "rg_lambda": rg_lambda,
            "da_lambda": da_lambda, "da_subln_g": da_subln_g,
            "w_out": w_out, "ln1_g": ln1_g, "ln1_b": ln1_b,
            "ca_wq": ca_wq, "ca_wk": ca_wk, "ca_wv": ca_wv, "ca_wo": ca_wo,
            "ln2_g": ln2_g, "ln2_b": ln2_b,
            "ffn_w_up": ffn_w_up, "ffn_conv_w": ffn_conv_w, "ffn_conv_b": ffn_conv_b,
            "ffn_w_down": ffn_w_down, "ln3_g": ln3_g, "ln3_b": ln3_b}


def reference(x, mem, positions, w_in, hg_lower_bounds, hg_norm_g, rg_conv_w, rg_conv_b, rg_wa, rg_ba,
              rg_wx, rg_bx, rg_lambda, da_lambda, da_subln_g, w_out, ln1_g, ln1_b,
              ca_wq, ca_wk, ca_wv, ca_wo, ln2_g, ln2_b, ffn_w_up, ffn_conv_w, ffn_conv_b,
              ffn_w_down, ln3_g, ln3_b):
    inv_freq = 1.0 / (ROPE_THETA ** (jnp.arange(0, DA_QK_DIM, 2, dtype=F32) / DA_QK_DIM))
    ang = positions.astype(F32)[..., None] * inv_freq
    cos, sin = jnp.cos(ang), jnp.sin(ang)

    lb_soft = jax.nn.softmax(hg_lower_bounds.astype(F32), axis=0)
    lb_all = jnp.cumsum(lb_soft, axis=0) - lb_soft[0:1]

    split_idx = [int(v) for v in np.cumsum(IN_SIZES)[:-1]]

    for layer in range(DEPTH):
        proj = x @ w_in[layer]
        hq, hf_f, hf_b, hi, hg, rx, ry, dq, dk, dv = jnp.split(proj, split_idx, axis=-1)
        o_hg = _hgrn2_mixer(hq, hf_f, hf_b, hi, hg, lb_all[layer], hg_norm_g[layer])
        o_rg = _rglru_mixer(rx, ry, rg_conv_w[layer], rg_conv_b[layer], rg_wa[layer], rg_ba[layer],
                            rg_wx[layer], rg_bx[layer], rg_lambda[layer])
        o_da = _diff_attention(dq, dk, dv, cos, sin, da_lambda[layer], da_subln_g[layer], layer)
        mix = jnp.concatenate([o_hg.astype(x.dtype), o_rg.astype(x.dtype), o_da.astype(x.dtype)], axis=-1) @ w_out[layer]
        x = _layer_norm(DEEPNORM_ALPHA * x + mix, ln1_g[layer], ln1_b[layer])
        ca = _memory_cross_attention(x, mem, ca_wq[layer], ca_wk[layer], ca_wv[layer], ca_wo[layer])
        x = _layer_norm(DEEPNORM_ALPHA * x + ca, ln2_g[layer], ln2_b[layer])
        ff = _conv_glu_ffn(x, ffn_w_up[layer], ffn_conv_w[layer], ffn_conv_b[layer], ffn_w_down[layer])
        x = _layer_norm(DEEPNORM_ALPHA * x + ff, ln3_g[layer], ln3_b[layer])
    return x
```

```python
import functools
import math

import jax
import jax.numpy as jnp
from jax import lax
from jax.experimental import pallas as pl
from jax.experimental.pallas import tpu as pltpu

F32 = jnp.float32
BF16 = jnp.bfloat16

D_MODEL = 1024
N_HEADS = 4
HG_WIDTH = 256
HG_DK = 64
RG_WIDTH = 256
RG_BLOCK = 64
RG_C = 8.0
DA_WIDTH = 512
DA_V_DIM = 128
DA_QK_DIM = 64
ROPE_THETA = 10000.0
CA_HEAD_DIM = 256
D_FF = 2816
LN_EPS = 1e-5
RMS_EPS = 1e-6

LANES = 128
SUBLANES = 8
VMEM_LIMIT = 56 * 1024 * 1024

NT_DIMS = (((1,), (1,)), ((), ()))
TN_DIMS = (((0,), (0,)), ((), ()))
NEG_BIG = -1e30


def _cparams(sem):
    return pltpu.CompilerParams(dimension_semantics=sem, vmem_limit_bytes=VMEM_LIMIT)


def _const_spec(shape):
    nd = len(shape)
    return pl.BlockSpec(shape, lambda *_: (0,) * nd, pipeline_mode=pl.Buffered(1))


def _layer_norm(y, g, b):
    mu = jnp.mean(y, axis=-1, keepdims=True)
    d = y - mu
    var = jnp.mean(d * d, axis=-1, keepdims=True)
    return d * lax.rsqrt(var + LN_EPS) * g + b


def _gelu_tanh(x):
    return 0.5 * x * (1.0 + jnp.tanh(math.sqrt(2.0 / math.pi) * (x + 0.044715 * (x * x * x))))


def _head_block_ones(n, blk):
    r = lax.broadcasted_iota(jnp.int32, (n, n), 0) // blk
    c = lax.broadcasted_iota(jnp.int32, (n, n), 1) // blk
    return r == c


def _inproj_kernel(x_ref, whg_ref, wrg_ref, wq_ref, wk_ref, wv_ref, cos_ref, s1_ref, s2_ref,
                   hg_ref, rg_ref, qm_ref, k_ref, v_ref):
    xb = x_ref[...].astype(BF16)
    hg_ref[...] = jnp.dot(xb, whg_ref[...], preferred_element_type=F32)
    rg_ref[...] = jnp.dot(xb, wrg_ref[...], preferred_element_type=F32)

    reps = DA_WIDTH // LANES
    c = jnp.concatenate([cos_ref[...]] * reps, axis=1)
    s1 = jnp.concatenate([s1_ref[...]] * reps, axis=1)
    s2 = jnp.concatenate([s2_ref[...]] * reps, axis=1)
    half = DA_QK_DIM // 2

    def rot(t):
        return t * c + pltpu.roll(t, DA_WIDTH - half, 1) * s1 + pltpu.roll(t, half, 1) * s2

    q = rot(jnp.dot(xb, wq_ref[...], preferred_element_type=F32)) * (DA_QK_DIM ** -0.5)
    lane = lax.broadcasted_iota(jnp.int32, q.shape, 1)
    first = (lane % DA_V_DIM) < DA_QK_DIM
    qm_ref[0] = jnp.where(first, q, 0.0).astype(BF16)
    qm_ref[1] = jnp.where(first, 0.0, q).astype(BF16)
    k_ref[...] = rot(jnp.dot(xb, wk_ref[...], preferred_element_type=F32)).astype(BF16)
    v_ref[...] = jnp.dot(xb, wv_ref[...], preferred_element_type=F32).astype(BF16)


def _inproj(x2, whg, wrg, wq, wk, wv, cos_t, s1_t, s2_t, tm):
    m = x2.shape[0]
    row = lambda w: pl.BlockSpec((tm, w), lambda i: (i, 0))
    return pl.pallas_call(
        _inproj_kernel,
        grid=(m // tm,),
        in_specs=[row(D_MODEL), _const_spec(whg.shape), _const_spec(wrg.shape), _const_spec(wq.shape),
                  _const_spec(wk.shape), _const_spec(wv.shape), row(LANES), row(LANES), row(LANES)],
        out_specs=[row(5 * HG_WIDTH), row(2 * RG_WIDTH),
                   pl.BlockSpec((2, tm, DA_WIDTH), lambda i: (0, i, 0)), row(DA_WIDTH), row(DA_WIDTH)],
        out_shape=[jax.ShapeDtypeStruct((m, 5 * HG_WIDTH), F32),
                   jax.ShapeDtypeStruct((m, 2 * RG_WIDTH), F32),
                   jax.ShapeDtypeStruct((2, m, DA_WIDTH), BF16),
                   jax.ShapeDtypeStruct((m, DA_WIDTH), BF16),
                   jax.ShapeDtypeStruct((m, DA_WIDTH), BF16)],
        compiler_params=_cparams(("parallel",)),
        name="inproj",
    )(x2, whg, wrg, wq, wk, wv, cos_t, s1_t, s2_t)


def _attn_kernel(lam_ref, qm_ref, k_ref, v_ref, g_ref, o_ref, m_sc, l_sc, acc_sc, *, tq, tk, nk, out_scale):
    q = qm_ref[...].reshape(2 * tq, DA_V_DIM)
    m_sc[...] = jnp.full(m_sc.shape, -jnp.inf, F32)
    l_sc[...] = jnp.zeros(l_sc.shape, F32)
    acc_sc[...] = jnp.zeros(acc_sc.shape, F32)
    reps = tk // LANES

    def body(i, carry):
        off = pl.multiple_of(i * tk, tk)
        kc = k_ref[pl.ds(off, tk), :]
        vc = v_ref[pl.ds(off, tk), :]
        s = lax.dot_general(q, kc, NT_DIMS, preferred_element_type=F32)
        m_prev = m_sc[...]
        m_next = jnp.maximum(m_prev, jnp.max(s, axis=1, keepdims=True))
        alpha = jnp.exp(m_prev - m_next)
        p = jnp.exp(s - jnp.concatenate([m_next] * reps, axis=1))
        l_sc[...] = alpha * l_sc[...] + jnp.sum(p, axis=1, keepdims=True)
        acc_sc[...] = alpha * acc_sc[...] + jnp.dot(p.astype(BF16), vc, preferred_element_type=F32)
        m_sc[...] = m_next
        return carry

    lax.fori_loop(0, nk, body, 0)
    o = acc_sc[...] / l_sc[...]
    lam = lam_ref[...]
    d = o[:tq] - lam * o[tq:]
    ms = jnp.mean(d * d, axis=1, keepdims=True)
    o_ref[...] = (d * lax.rsqrt(ms + RMS_EPS) * g_ref[...] * out_scale).astype(o_ref.dtype)


def _diff_attention(lam_row, qm, kk, vv, subln_g, batch, seq, out_scale, tq, tk):
    m = batch * seq
    nq = seq // tq
    kern = functools.partial(_attn_kernel, tq=tq, tk=tk, nk=seq // tk, out_scale=out_scale)
    return pl.pallas_call(
        kern,
        grid=(batch, N_HEADS, nq),
        in_specs=[pl.BlockSpec((1, LANES), lambda b, h, i: (0, 0)),
                  pl.BlockSpec((2, tq, DA_V_DIM), lambda b, h, i: (0, b * nq + i, h)),
                  pl.BlockSpec((seq, DA_V_DIM), lambda b, h, i: (b, h)),
                  pl.BlockSpec((seq, DA_V_DIM), lambda b, h, i: (b, h)),
                  pl.BlockSpec((1, DA_V_DIM), lambda b, h, i: (0, 0))],
        out_specs=pl.BlockSpec((tq, DA_V_DIM), lambda b, h, i: (b * nq + i, h)),
        out_shape=jax.ShapeDtypeStruct((m, DA_WIDTH), BF16),
        scratch_shapes=[pltpu.VMEM((2 * tq, LANES), F32), pltpu.VMEM((2 * tq, LANES), F32),
                        pltpu.VMEM((2 * tq, DA_V_DIM), F32)],
        compiler_params=_cparams(("parallel", "parallel", "arbitrary")),
        name="diff_attn",
    )(lam_row, qm, kk, vv, subln_g)


def _hgrn_kernel(q_ref, f_ref, v_ref, lb_ref, o_ref, st_sc, *, tb, chunk, reverse):
    @pl.when(pl.program_id(1) == 0)
    def _():
        st_sc[...] = jnp.zeros(st_sc.shape, F32)

    lb = lb_ref[...]
    z = f_ref[...]
    qraw = q_ref[...]
    qs = qraw * jax.nn.sigmoid(qraw)
    g = jnp.log(lb + (1.0 - lb) * jax.nn.sigmoid(z))
    kfull = (1.0 - lb) * jax.nn.sigmoid(-z)
    vfull = v_ref[...]

    r = lax.broadcasted_iota(jnp.int32, (tb, tb), 0)
    c = lax.broadcasted_iota(jnp.int32, (tb, tb), 1)
    tri = ((r // chunk) == (c // chunk)) & ((c >= r) if reverse else (c <= r))
    b = jnp.dot(tri.astype(F32), g, precision=lax.Precision.HIGHEST, preferred_element_type=F32)

    same_head = _head_block_ones(HG_WIDTH, HG_DK)
    bones = same_head.astype(BF16)
    rows = lax.broadcasted_iota(jnp.int32, (chunk, HG_WIDTH), 0)

    order = range(tb // chunk - 1, -1, -1) if reverse else range(tb // chunk)
    for ci in order:
        sl = slice(ci * chunk, (ci + 1) * chunk)
        bc, qc, kc, vc = b[sl], qs[sl], kfull[sl], vfull[sl]
        b_end = bc[0:1] if reverse else bc[chunk - 1:chunk]
        st = st_sc[...]
        qt = (qc * jnp.exp(bc)).astype(BF16)
        o_c = lax.dot_general(qt, st.astype(BF16), NT_DIMS, preferred_element_type=F32)
        pieces = []
        for s in range(chunk):
            keep = (rows <= s) if reverse else (rows >= s)
            e = jnp.exp(jnp.where(keep, bc - bc[s:s + 1], NEG_BIG))
            pieces.append((qc * e * kc[s:s + 1]).astype(BF16))
        att = jnp.dot(jnp.concatenate(pieces, axis=0), bones, preferred_element_type=F32)
        for s in range(chunk):
            o_c = o_c + att[s * chunk:(s + 1) * chunk] * vc[s:s + 1]
        o_ref[sl, :] = o_c
        kh = (kc * jnp.exp(b_end - bc)).astype(BF16)
        kv = lax.dot_general(vc.astype(BF16), kh, TN_DIMS, preferred_element_type=F32)
        st_sc[...] = jnp.where(same_head, st * jnp.exp(b_end) + kv, 0.0)


def _hgrn(hg_in, lb_row, batch, seq, reverse, tb, chunk):
    m = batch * seq
    nb = seq // tb
    pos = (lambda j: nb - 1 - j) if reverse else (lambda j: j)
    col = lambda cidx: pl.BlockSpec((tb, HG_WIDTH), lambda b, j: (b * nb + pos(j), cidx))
    kern = functools.partial(_hgrn_kernel, tb=tb, chunk=chunk, reverse=reverse)
    return pl.pallas_call(
        kern,
        grid=(batch, nb),
        in_specs=[col(0), col(2 if reverse else 1), col(3), pl.BlockSpec((1, HG_WIDTH), lambda b, j: (0, 0))],
        out_specs=pl.BlockSpec((tb, HG_WIDTH), lambda b, j: (b * nb + pos(j), 0)),
        out_shape=jax.ShapeDtypeStruct((m, HG_WIDTH), F32),
        scratch_shapes=[pltpu.VMEM((HG_WIDTH, HG_WIDTH), F32)],
        compiler_params=_cparams(("parallel", "arbitrary")),
        name="hgrn_bwd" if reverse else "hgrn_fwd",
    )(hg_in, hg_in, hg_in, lb_row)


def _rglru_kernel(xp_ref, x_ref, xn_ref, cw_ref, cb_ref, wg_ref, bg_ref, nsp_ref, h_ref, carry_sc,
                  *, tb, nb, reverse):
    j = pl.program_id(1)
    pos = (nb - 1 - j) if reverse else j

    @pl.when(j == 0)
    def _():
        carry_sc[...] = jnp.zeros(carry_sc.shape, F32)

    xp = jnp.where(pos > 0, xp_ref[...], 0.0)
    xn = jnp.where(pos < nb - 1, xn_ref[...], 0.0)
    xe = jnp.concatenate([xp, x_ref[...], xn], axis=0)
    cw = cw_ref[...]
    o = SUBLANES - 2
    xc = (cw[0:1] * xe[o:o + tb] + cw[1:2] * xe[o + 1:o + 1 + tb] + cw[2:3] * xe[o + 2:o + 2 + tb]
          + cw[3:4] * xe[o + 3:o + 3 + tb] + cb_ref[...])
    gates = jnp.dot(xc.astype(BF16), wg_ref[...], preferred_element_type=F32) + bg_ref[...]
    rgate = jax.nn.sigmoid(gates[:, :RG_WIDTH])
    igate = jax.nn.sigmoid(gates[:, RG_WIDTH:])
    log_a = nsp_ref[...] * rgate
    a = jnp.exp(log_a)
    u = jnp.sqrt(1.0 - a * a) * (igate * xc)

    rows = lax.broadcasted_iota(jnp.int32, (tb, RG_WIDTH), 0)
    d = 1
    while d < tb:
        if reverse:
            a_sh, u_sh, ok = pltpu.roll(a, tb - d, 0), pltpu.roll(u, tb - d, 0), rows < tb - d
        else:
            a_sh, u_sh, ok = pltpu.roll(a, d, 0), pltpu.roll(u, d, 0), rows >= d
        u = jnp.where(ok, a * u_sh + u, u)
        a = jnp.where(ok, a * a_sh, a)
        d *= 2
    h = a * carry_sc[...] + u
    h_ref[...] = h
    carry_sc[...] = h[0:1] if reverse else h[tb - 1:tb]


def _rglru(rg_in, conv_w, conv_b, wg, bg, nsp, batch, seq, reverse, tb):
    m = batch * seq
    nb = seq // tb
    per8 = tb // SUBLANES
    last8 = m // SUBLANES - 1
    pos = (lambda j: nb - 1 - j) if reverse else (lambda j: j)
    blk = lambda b, j: b * nb + pos(j)
    kern = functools.partial(_rglru_kernel, tb=tb, nb=nb, reverse=reverse)
    return pl.pallas_call(
        kern,
        grid=(batch, nb),
        in_specs=[pl.BlockSpec((SUBLANES, RG_WIDTH), lambda b, j: (jnp.maximum(blk(b, j) * per8 - 1, 0), 0)),
                  pl.BlockSpec((tb, RG_WIDTH), lambda b, j: (blk(b, j), 0)),
                  pl.BlockSpec((SUBLANES, RG_WIDTH), lambda b, j: (jnp.minimum((blk(b, j) + 1) * per8, last8), 0)),
                  pl.BlockSpec(conv_w.shape, lambda b, j: (0, 0)),
                  pl.BlockSpec(conv_b.shape, lambda b, j: (0, 0)),
                  pl.BlockSpec(wg.shape, lambda b, j: (0, 0)),
                  pl.BlockSpec(bg.shape, lambda b, j: (0, 0)),
                  pl.BlockSpec(nsp.shape, lambda b, j: (0, 0))],
        out_specs=pl.BlockSpec((tb, RG_WIDTH), lambda b, j: (blk(b, j), 0)),
        out_shape=jax.ShapeDtypeStruct((m, RG_WIDTH), F32),
        scratch_shapes=[pltpu.VMEM((1, RG_WIDTH), F32)],
        compiler_params=_cparams(("parallel", "arbitrary")),
        name="rglru_bwd" if reverse else "rglru_fwd",
    )(rg_in, rg_in, rg_in, conv_w, conv_b, wg, bg, nsp)


def _mix_kernel(x_ref, of_ref, ob_ref, hgg_ref, ng_ref, hf_ref, hb_ref, ry_ref, da_ref, wout_ref,
                lg_ref, lb_ref, out_ref, *, alpha):
    o = of_ref[...] + ob_ref[...]
    bones = _head_block_ones(HG_WIDTH, HG_DK).astype(BF16)
    sq = o * o
    hi = sq.astype(BF16)
    lo = (sq - hi.astype(F32)).astype(BF16)
    ms = (jnp.dot(hi, bones, preferred_element_type=F32)
          + jnp.dot(lo, bones, preferred_element_type=F32)) * (1.0 / HG_DK)
    gate = hgg_ref[...]
    o_hg = o * lax.rsqrt(ms + RMS_EPS) * ng_ref[...] * (gate * jax.nn.sigmoid(gate))
    o_rg = (hf_ref[...] + hb_ref[...]) * _gelu_tanh(ry_ref[...])
    cat = jnp.concatenate([o_hg.astype(BF16), o_rg.astype(BF16), da_ref[...]], axis=1)
    mix = jnp.dot(cat, wout_ref[...], preferred_element_type=F32)
    out_ref[...] = _layer_norm(alpha * x_ref[...] + mix, lg_ref[...], lb_ref[...])


def _mix(x2, o_f, o_b, hg_in, norm_g, h_f, h_b, rg_in, o_da, w_out, ln_g, ln_b, alpha, tm):
    m = x2.shape[0]
    row = lambda w, cidx=0: pl.BlockSpec((tm, w), lambda i: (i, cidx))
    vec = lambda w: pl.BlockSpec((1, w), lambda i: (0, 0))
    return pl.pallas_call(
        functools.partial(_mix_kernel, alpha=alpha),
        grid=(m // tm,),
        in_specs=[row(D_MODEL), row(HG_WIDTH), row(HG_WIDTH), row(HG_WIDTH, 4), vec(HG_WIDTH),
                  row(RG_WIDTH), row(RG_WIDTH), row(RG_WIDTH, 1), row(DA_WIDTH),
                  _const_spec(w_out.shape), vec(D_MODEL), vec(D_MODEL)],
        out_specs=row(D_MODEL),
        out_shape=jax.ShapeDtypeStruct((m, D_MODEL), F32),
        compiler_params=_cparams(("parallel",)),
        name="mix_out",
    )(x2, o_f, o_b, hg_in, norm_g, h_f, h_b, rg_in, o_da, w_out, ln_g, ln_b)


def _matmul_kernel(a_ref, w_ref, o_ref):
    o_ref[...] = jnp.dot(a_ref[...].astype(BF16), w_ref[...], preferred_element_type=F32).astype(o_ref.dtype)


def _mem_proj(mem2, w):
    m, n = mem2.shape[0], w.shape[1]
    return pl.pallas_call(
        _matmul_kernel,
        grid=(1,),
        in_specs=[pl.BlockSpec(mem2.shape, lambda i: (0, 0)), pl.BlockSpec(w.shape, lambda i: (0, 0))],
        out_specs=pl.BlockSpec((m, n), lambda i: (0, 0)),
        out_shape=jax.ShapeDtypeStruct((m, n), BF16),
        compiler_params=_cparams(("arbitrary",)),
        name="mem_proj",
    )(mem2, w)


def _ca_kernel(x_ref, wq_ref, k_ref, v_ref, wo_ref, lg_ref, lb_ref, out_ref, *, alpha):
    x = x_ref[...]
    q = jnp.dot(x.astype(BF16), wq_ref[...], preferred_element_type=F32) * (CA_HEAD_DIM ** -0.5)
    outs = []
    for h in range(N_HEADS):
        sl = slice(h * CA_HEAD_DIM, (h + 1) * CA_HEAD_DIM)
        s = lax.dot_general(q[:, sl].astype(BF16), k_ref[:, sl], NT_DIMS, preferred_element_type=F32)
        p = jnp.exp(s - jnp.max(s, axis=1, keepdims=True))
        l = jnp.sum(p, axis=1, keepdims=True)
        o = jnp.dot(p.astype(BF16), v_ref[:, sl], preferred_element_type=F32) / l
        outs.append(o.astype(BF16))
    ca = jnp.dot(jnp.concatenate(outs, axis=1), wo_ref[...], preferred_element_type=F32)
    out_ref[...] = _layer_norm(alpha * x + ca, lg_ref[...], lb_ref[...])


def _cross_attention(x2, wq, kmem, vmem, wo, ln_g, ln_b, batch, seq, n_mem, alpha, tm):
    m = x2.shape[0]
    nt = seq // tm
    vec = pl.BlockSpec((1, D_MODEL), lambda b, i: (0, 0))
    return pl.pallas_call(
        functools.partial(_ca_kernel, alpha=alpha),
        grid=(batch, nt),
        in_specs=[pl.BlockSpec((tm, D_MODEL), lambda b, i: (b * nt + i, 0)),
                  _const_spec(wq.shape),
                  pl.BlockSpec((n_mem, D_MODEL), lambda b, i: (b, 0)),
                  pl.BlockSpec((n_mem, D_MODEL), lambda b, i: (b, 0)),
                  _const_spec(wo.shape), vec, vec],
        out_specs=pl.BlockSpec((tm, D_MODEL), lambda b, i: (b * nt + i, 0)),
        out_shape=jax.ShapeDtypeStruct((m, D_MODEL), F32),
        compiler_params=_cparams(("parallel", "parallel")),
        name="cross_attn",
    )(x2, wq, kmem, vmem, wo, ln_g, ln_b)


def _ffn_kernel(xp_ref, x_ref, xn_ref, wg_ref, wv_ref, cw_ref, cb_ref, wd_ref, lg_ref, lb_ref, out_ref,
                *, tm, nt, alpha):
    pos = pl.program_id(1)
    x = x_ref[...]
    xp = jnp.where(pos > 0, xp_ref[...], 0.0)
    xn = jnp.where(pos < nt - 1, xn_ref[...], 0.0)
    xe = jnp.concatenate([xp, x, xn], axis=0).astype(BF16)
    gate = jnp.dot(xe, wg_ref[...], preferred_element_type=F32)
    cw = cw_ref[...]
    o = SUBLANES - 1
    gc = (cw[0:1] * gate[o:o + tm] + cw[1:2] * gate[o + 1:o + 1 + tm] + cw[2:3] * gate[o + 2:o + 2 + tm]
          + cb_ref[...])
    val = jnp.dot(x.astype(BF16), wv_ref[...], preferred_element_type=F32)
    hid = (_gelu_tanh(gc) * val).astype(BF16)
    ff = jnp.dot(hid, wd_ref[...], preferred_element_type=F32)
    out_ref[...] = _layer_norm(alpha * x + ff, lg_ref[...], lb_ref[...])


def _ffn(x2, wg, wv, conv_w, conv_b, wd, ln_g, ln_b, batch, seq, alpha, tm):
    m = x2.shape[0]
    nt = seq // tm
    per8 = tm // SUBLANES
    last8 = m // SUBLANES - 1
    blk = lambda b, i: b * nt + i
    vec = pl.BlockSpec((1, D_MODEL), lambda b, i: (0, 0))
    return pl.pallas_call(
        functools.partial(_ffn_kernel, tm=tm, nt=nt, alpha=alpha),
        grid=(batch, nt),
        in_specs=[pl.BlockSpec((SUBLANES, D_MODEL), lambda b, i: (jnp.maximum(blk(b, i) * per8 - 1, 0), 0)),
                  pl.BlockSpec((tm, D_MODEL), lambda b, i: (blk(b, i), 0)),
                  pl.BlockSpec((SUBLANES, D_MODEL), lambda b, i: (jnp.minimum((blk(b, i) + 1) * per8, last8), 0)),
                  _const_spec(wg.shape), _const_spec(wv.shape),
                  pl.BlockSpec(conv_w.shape, lambda b, i: (0, 0)),
                  pl.BlockSpec(conv_b.shape, lambda b, i: (0, 0)),
                  _const_spec(wd.shape), vec, vec],
        out_specs=pl.BlockSpec((tm, D_MODEL), lambda b, i: (blk(b, i), 0)),
        out_shape=jax.ShapeDtypeStruct((m, D_MODEL), F32),
        compiler_params=_cparams(("parallel", "parallel")),
        name="conv_glu",
    )(x2, x2, x2, wg, wv, conv_w, conv_b, wd, ln_g, ln_b)


def _block_diag(w):
    n, blk, _ = w.shape
    eye = jnp.eye(n, dtype=w.dtype)
    return jnp.einsum("nij,nm->nimj", w, eye).reshape(n * blk, n * blk)


def _pick(seq, pref):
    t = min(seq, pref)
    assert seq % t == 0, (seq, pref)
    return t


def kernel(x, mem, positions, w_in, hg_lower_bounds, hg_norm_g, rg_conv_w, rg_conv_b, rg_wa, rg_ba, rg_wx, rg_bx, rg_lambda, da_lambda, da_subln_g, w_out, ln1_g, ln1_b, ca_wq, ca_wk, ca_wv, ca_wo, ln2_g, ln2_b, ffn_w_up, ffn_conv_w, ffn_conv_b, ffn_w_down, ln3_g, ln3_b):
    batch, seq, _ = x.shape
    n_mem = mem.shape[1]
    depth = w_in.shape[0]
    alpha = (2 * depth) ** 0.25
    m = batch * seq

    tm_proj = _pick(seq, 512)
    tm_ffn = _pick(seq, 256)
    tq = _pick(seq, 256)
    tk = _pick(seq, 512)
    tb_hg = _pick(seq, 128)
    tb_rg = _pick(seq, 256)

    inv_freq = 1.0 / (ROPE_THETA ** (jnp.arange(0, DA_QK_DIM, 2, dtype=F32) / DA_QK_DIM))
    ang = positions.astype(F32).reshape(m, 1) * inv_freq
    cos, sin = jnp.cos(ang), jnp.sin(ang)
    zero = jnp.zeros_like(sin)
    cos_t = jnp.tile(cos, (1, 4))
    s1_t = jnp.tile(jnp.concatenate([-sin, zero], axis=1), (1, 2))
    s2_t = jnp.tile(jnp.concatenate([zero, sin], axis=1), (1, 2))

    lb_soft = jax.nn.softmax(hg_lower_bounds.astype(F32), axis=0)
    lb_all = jnp.cumsum(lb_soft, axis=0) - lb_soft[0:1]

    x2 = x.reshape(m, D_MODEL)
    mem2 = mem.reshape(batch * n_mem, D_MODEL)
    row = lambda v: v.reshape(1, -1).astype(F32)

    for layer in range(depth):
        w = w_in[layer].astype(BF16)
        o0 = 5 * HG_WIDTH
        o1 = o0 + 2 * RG_WIDTH
        hg_in, rg_in, qm, kk, vv = _inproj(
            x2, w[:, :o0], w[:, o0:o1], w[:, o1:o1 + DA_WIDTH], w[:, o1 + DA_WIDTH:o1 + 2 * DA_WIDTH],
            w[:, o1 + 2 * DA_WIDTH:], cos_t, s1_t, s2_t, tm_proj)

        lam_init = 0.8 - 0.6 * math.exp(-0.3 * layer)
        lp = da_lambda[layer].astype(F32)
        lam = jnp.exp(jnp.sum(lp[0] * lp[1])) - jnp.exp(jnp.sum(lp[2] * lp[3])) + lam_init
        o_da = _diff_attention(jnp.full((1, LANES), lam, F32), qm, kk, vv, row(da_subln_g[layer]),
                               batch, seq, 1.0 - lam_init, tq, tk)

        o_f = _hgrn(hg_in, row(lb_all[layer, 0]), batch, seq, False, tb_hg, 16)
        o_b = _hgrn(hg_in, row(lb_all[layer, 1]), batch, seq, True, tb_hg, 16)

        nsp = -RG_C * jax.nn.softplus(-rg_lambda[layer].astype(F32))
        h_dirs = []
        for d in range(2):
            wg = jnp.concatenate([_block_diag(rg_wa[layer, d]), _block_diag(rg_wx[layer, d])], axis=1).astype(BF16)
            bg = jnp.concatenate([rg_ba[layer, d], rg_bx[layer, d]]).reshape(1, -1).astype(F32)
            h_dirs.append(_rglru(rg_in, rg_conv_w[layer].astype(F32), row(rg_conv_b[layer]), wg, bg,
                                 row(nsp[d]), batch, seq, d == 1, tb_rg))

        x2 = _mix(x2, o_f, o_b, hg_in, row(hg_norm_g[layer]), h_dirs[0], h_dirs[1], rg_in, o_da,
                  w_out[layer].astype(BF16), row(ln1_g[layer]), row(ln1_b[layer]), alpha, tm_proj)

        kmem = _mem_proj(mem2, ca_wk[layer].astype(BF16))
        vmem = _mem_proj(mem2, ca_wv[layer].astype(BF16))
        x2 = _cross_attention(x2, ca_wq[layer].astype(BF16), kmem, vmem, ca_wo[layer].astype(BF16),
                              row(ln2_g[layer]), row(ln2_b[layer]), batch, seq, n_mem, alpha, tm_proj)

        w_up = ffn_w_up[layer].astype(BF16)
        x2 = _ffn(x2, w_up[:, :D_FF], w_up[:, D_FF:], ffn_conv_w[layer].astype(F32), row(ffn_conv_b[layer]),
                  ffn_w_down[layer].astype(BF16), row(ln3_g[layer]), row(ln3_b[layer]), batch, seq, alpha, tm_ffn)

    return x2.reshape(batch, seq, D_MODEL)
```

```python
import functools
import math

import jax
import jax.numpy as jnp
from jax import lax
from jax.experimental import pallas as pl
from jax.experimental.pallas import tpu as pltpu

F32 = jnp.float32
BF16 = jnp.bfloat16

D_MODEL = 1024
N_HEADS = 4
HG_WIDTH = 256
HG_DK = 64
RG_WIDTH = 256
RG_BLOCK = 64
RG_C = 8.0
DA_WIDTH = 512
DA_V_DIM = 128
DA_QK_DIM = 64
ROPE_THETA = 10000.0
CA_HEAD_DIM = 256
D_FF = 2816
LN_EPS = 1e-5
RMS_EPS = 1e-6

LANES = 128
SUBLANES = 8
VMEM_LIMIT = 56 * 1024 * 1024

NT_DIMS = (((1,), (1,)), ((), ()))
TN_DIMS = (((0,), (0,)), ((), ()))
NEG_BIG = -1e30
LOG2_E = math.log2(math.e)
ATTN_UNROLL = 8
BF16_ROWS = 16
VT_ROWS = DA_V_DIM + BF16_ROWS


def _cparams(sem, flags=None):
    return pltpu.CompilerParams(dimension_semantics=sem, vmem_limit_bytes=VMEM_LIMIT, flags=flags)


def _const_spec(shape):
    nd = len(shape)
    return pl.BlockSpec(shape, lambda *_: (0,) * nd, pipeline_mode=pl.Buffered(1))


def _layer_norm(y, g, b):
    mu = jnp.mean(y, axis=-1, keepdims=True)
    d = y - mu
    var = jnp.mean(d * d, axis=-1, keepdims=True)
    return d * lax.rsqrt(var + LN_EPS) * g + b


def _gelu_tanh(x):
    return 0.5 * x * (1.0 + jnp.tanh(math.sqrt(2.0 / math.pi) * (x + 0.044715 * (x * x * x))))


def _head_block_ones(n, blk):
    r = lax.broadcasted_iota(jnp.int32, (n, n), 0) // blk
    c = lax.broadcasted_iota(jnp.int32, (n, n), 1) // blk
    return r == c


def _inproj_kernel(x_ref, whg_ref, wrg_ref, wq_ref, wk_ref, wv_ref, cos_ref, s1_ref, s2_ref,
                   hg_ref, rg_ref, qm_ref, k_ref, v_ref):
    xb = x_ref[...].astype(BF16)
    hg_ref[...] = jnp.dot(xb, whg_ref[...], preferred_element_type=F32)
    rg_ref[...] = jnp.dot(xb, wrg_ref[...], preferred_element_type=F32)

    reps = DA_WIDTH // LANES
    c = jnp.concatenate([cos_ref[...]] * reps, axis=1)
    s1 = jnp.concatenate([s1_ref[...]] * reps, axis=1)
    s2 = jnp.concatenate([s2_ref[...]] * reps, axis=1)
    half = DA_QK_DIM // 2

    def rot(t):
        return t * c + pltpu.roll(t, DA_WIDTH - half, 1) * s1 + pltpu.roll(t, half, 1) * s2

    q = rot(jnp.dot(xb, wq_ref[...], preferred_element_type=F32)) * (DA_QK_DIM ** -0.5 * LOG2_E)
    qt = q.T
    sub = lax.broadcasted_iota(jnp.int32, qt.shape, 0)
    first = (sub % DA_V_DIM) < DA_QK_DIM
    qm_ref[0] = jnp.where(first, qt, 0.0).astype(BF16)
    qm_ref[1] = jnp.where(first, 0.0, qt).astype(BF16)
    k_ref[...] = rot(jnp.dot(xb, wk_ref[...], preferred_element_type=F32)).astype(BF16)
    vt = jnp.dot(xb, wv_ref[...], preferred_element_type=F32).T.astype(BF16)
    ones = jnp.ones((VT_ROWS - DA_V_DIM, vt.shape[1]), BF16)
    for h in range(N_HEADS):
        v_ref[0, h * VT_ROWS:h * VT_ROWS + DA_V_DIM, :] = vt[h * DA_V_DIM:(h + 1) * DA_V_DIM]
        v_ref[0, h * VT_ROWS + DA_V_DIM:(h + 1) * VT_ROWS, :] = ones


def _inproj(x2, whg, wrg, wq, wk, wv, cos_t, s1_t, s2_t, tm):
    m = x2.shape[0]
    row = lambda w: pl.BlockSpec((tm, w), lambda i: (i, 0))
    return pl.pallas_call(
        _inproj_kernel,
        grid=(m // tm,),
        in_specs=[row(D_MODEL), _const_spec(whg.shape), _const_spec(wrg.shape), _const_spec(wq.shape),
                  _const_spec(wk.shape), _const_spec(wv.shape), row(LANES), row(LANES), row(LANES)],
        out_specs=[row(5 * HG_WIDTH), row(2 * RG_WIDTH),
                   pl.BlockSpec((2, DA_WIDTH, tm), lambda i: (0, 0, i)), row(DA_WIDTH),
                   pl.BlockSpec((1, N_HEADS * VT_ROWS, tm), lambda i: (i, 0, 0))],
        out_shape=[jax.ShapeDtypeStruct((m, 5 * HG_WIDTH), F32),
                   jax.ShapeDtypeStruct((m, 2 * RG_WIDTH), F32),
                   jax.ShapeDtypeStruct((2, DA_WIDTH, m), BF16),
                   jax.ShapeDtypeStruct((m, DA_WIDTH), BF16),
                   jax.ShapeDtypeStruct((m // tm, N_HEADS * VT_ROWS, tm), BF16)],
        compiler_params=_cparams(("parallel",)),
        name="inproj",
    )(x2, whg, wrg, wq, wk, wv, cos_t, s1_t, s2_t)


def _attn_kernel(lam_ref, qt_ref, k_ref, vt_ref, g_ref, o_ref, s_sc, acc_sc, *, tq, tk, nk, out_scale):
    qt = jnp.concatenate([qt_ref[0], qt_ref[1]], axis=1)
    acc_sc[...] = jnp.zeros(acc_sc.shape, F32)

    def scores(idx, slot):
        kc = k_ref[pl.ds(pl.multiple_of(idx * tk, tk), tk), :]
        s_sc[slot] = jnp.dot(kc, qt, preferred_element_type=F32)

    def absorb(idx, slot, m_prev):
        s = s_sc[slot]
        m_next = jnp.maximum(m_prev, jnp.max(s, axis=0, keepdims=True))
        p = jnp.exp2(s - m_next).astype(BF16)
        pv = jnp.dot(vt_ref[idx], p, preferred_element_type=F32)
        acc_sc[...] = jnp.exp2(m_prev - m_next) * acc_sc[...] + pv
        return m_next

    scores(0, 0)

    def body(t, m_run):
        for u in range(ATTN_UNROLL):
            c = ATTN_UNROLL * t + u
            scores(jnp.minimum(c + 1, nk - 1), (u + 1) % 2)
            m_run = absorb(c, u % 2, m_run)
        return m_run

    lax.fori_loop(0, nk // ATTN_UNROLL, body, jnp.full((1, 2 * tq), -jnp.inf, F32))
    o = acc_sc[:DA_V_DIM, :] / acc_sc[DA_V_DIM:DA_V_DIM + 1, :]
    d = o[:, :tq] - lam_ref[...][:, :1] * o[:, tq:]
    ms = jnp.mean(d * d, axis=0, keepdims=True)
    dn = (d * lax.rsqrt(ms + RMS_EPS)).T
    o_ref[...] = (dn * g_ref[...] * out_scale).astype(o_ref.dtype)


def _diff_attention(lam_row, qt, kk, vt, subln_g, batch, seq, out_scale, tq, tk):
    m = batch * seq
    nq = seq // tq
    nk = seq // tk
    assert nk % ATTN_UNROLL == 0 and ATTN_UNROLL % 2 == 0, nk
    kern = functools.partial(_attn_kernel, tq=tq, tk=tk, nk=nk, out_scale=out_scale)
    return pl.pallas_call(
        kern,
        grid=(batch, N_HEADS, nq),
        in_specs=[pl.BlockSpec((1, LANES), lambda b, h, i: (0, 0)),
                  pl.BlockSpec((2, DA_V_DIM, tq), lambda b, h, i: (0, h, b * nq + i)),
                  pl.BlockSpec((seq, DA_V_DIM), lambda b, h, i: (b, h)),
                  pl.BlockSpec((nk, VT_ROWS, tk), lambda b, h, i: (b, h, 0)),
                  pl.BlockSpec((1, DA_V_DIM), lambda b, h, i: (0, 0))],
        out_specs=pl.BlockSpec((tq, DA_V_DIM), lambda b, h, i: (b * nq + i, h)),
        out_shape=jax.ShapeDtypeStruct((m, DA_WIDTH), BF16),
        scratch_shapes=[pltpu.VMEM((2, tk, 2 * tq), F32), pltpu.VMEM((VT_ROWS, 2 * tq), F32)],
        compiler_params=_cparams(("parallel", "parallel", "arbitrary")),
        name="diff_attn",
    )(lam_row, qt, kk, vt, subln_g)


def _hgrn_kernel(q_ref, f_ref, v_ref, lb_ref, o_ref, st_sc, *, tb, chunk, reverse):
    @pl.when(pl.program_id(1) == 0)
    def _():
        st_sc[...] = jnp.zeros(st_sc.shape, F32)

    lb = lb_ref[...]
    z = f_ref[...]
    qraw = q_ref[...]
    qs = qraw * jax.nn.sigmoid(qraw)
    g = jnp.log(lb + (1.0 - lb) * jax.nn.sigmoid(z))
    kfull = (1.0 - lb) * jax.nn.sigmoid(-z)
    vfull = v_ref[...]

    r = lax.broadcasted_iota(jnp.int32, (tb, tb), 0)
    c = lax.broadcasted_iota(jnp.int32, (tb, tb), 1)
    tri = ((r // chunk) == (c // chunk)) & ((c >= r) if reverse else (c <= r))
    b = jnp.dot(tri.astype(F32), g, precision=lax.Precision.HIGHEST, preferred_element_type=F32)

    same_head = _head_block_ones(HG_WIDTH, HG_DK)
    bones = same_head.astype(BF16)
    rows = lax.broadcasted_iota(jnp.int32, (chunk, HG_WIDTH), 0)

    order = range(tb // chunk - 1, -1, -1) if reverse else range(tb // chunk)
    for ci in order:
        sl = slice(ci * chunk, (ci + 1) * chunk)
        bc, qc, kc, vc = b[sl], qs[sl], kfull[sl], vfull[sl]
        b_end = bc[0:1] if reverse else bc[chunk - 1:chunk]
        st = st_sc[...]
        qt = (qc * jnp.exp(bc)).astype(BF16)
        o_c = lax.dot_general(qt, st.astype(BF16), NT_DIMS, preferred_element_type=F32)
        pieces = []
        for s in range(chunk):
            keep = (rows <= s) if reverse else (rows >= s)
            e = jnp.exp(jnp.where(keep, bc - bc[s:s + 1], NEG_BIG))
            pieces.append((qc * e * kc[s:s + 1]).astype(BF16))
        att = jnp.dot(jnp.concatenate(pieces, axis=0), bones, preferred_element_type=F32)
        for s in range(chunk):
            o_c = o_c + att[s * chunk:(s + 1) * chunk] * vc[s:s + 1]
        o_ref[sl, :] = o_c
        kh = (kc * jnp.exp(b_end - bc)).astype(BF16)
        kv = lax.dot_general(vc.astype(BF16), kh, TN_DIMS, preferred_element_type=F32)
        st_sc[...] = jnp.where(same_head, st * jnp.exp(b_end) + kv, 0.0)


def _hgrn(hg_in, lb_row, batch, seq, reverse, tb, chunk):
    m = batch * seq
    nb = seq // tb
    pos = (lambda j: nb - 1 - j) if reverse else (lambda j: j)
    col = lambda cidx: pl.BlockSpec((tb, HG_WIDTH), lambda b, j: (b * nb + pos(j), cidx))
    kern = functools.partial(_hgrn_kernel, tb=tb, chunk=chunk, reverse=reverse)
    return pl.pallas_call(
        kern,
        grid=(batch, nb),
        in_specs=[col(0), col(2 if reverse else 1), col(3), pl.BlockSpec((1, HG_WIDTH), lambda b, j: (0, 0))],
        out_specs=pl.BlockSpec((tb, HG_WIDTH), lambda b, j: (b * nb + pos(j), 0)),
        out_shape=jax.ShapeDtypeStruct((m, HG_WIDTH), F32),
        scratch_shapes=[pltpu.VMEM((HG_WIDTH, HG_WIDTH), F32)],
        compiler_params=_cparams(("parallel", "arbitrary")),
        name="hgrn_bwd" if reverse else "hgrn_fwd",
    )(hg_in, hg_in, hg_in, lb_row)


def _rglru_kernel(xp_ref, x_ref, xn_ref, cw_ref, cb_ref, wg_ref, bg_ref, nsp_ref, h_ref, carry_sc,
                  *, tb, nb, reverse):
    j = pl.program_id(1)
    pos = (nb - 1 - j) if reverse else j

    @pl.when(j == 0)
    def _():
        carry_sc[...] = jnp.zeros(carry_sc.shape, F32)

    xp = jnp.where(pos > 0, xp_ref[...], 0.0)
    xn = jnp.where(pos < nb - 1, xn_ref[...], 0.0)
    xe = jnp.concatenate([xp, x_ref[...], xn], axis=0)
    cw = cw_ref[...]
    o = SUBLANES - 2
    xc = (cw[0:1] * xe[o:o + tb] + cw[1:2] * xe[o + 1:o + 1 + tb] + cw[2:3] * xe[o + 2:o + 2 + tb]
          + cw[3:4] * xe[o + 3:o + 3 + tb] + cb_ref[...])
    gates = jnp.dot(xc.astype(BF16), wg_ref[...], preferred_element_type=F32) + bg_ref[...]
    rgate = jax.nn.sigmoid(gates[:, :RG_WIDTH])
    igate = jax.nn.sigmoid(gates[:, RG_WIDTH:])
    log_a = nsp_ref[...] * rgate
    a = jnp.exp(log_a)
    u = jnp.sqrt(1.0 - a * a) * (igate * xc)

    rows = lax.broadcasted_iota(jnp.int32, (tb, RG_WIDTH), 0)
    d = 1
    while d < tb:
        if reverse:
            a_sh, u_sh, ok = pltpu.roll(a, tb - d, 0), pltpu.roll(u, tb - d, 0), rows < tb - d
        else:
            a_sh, u_sh, ok = pltpu.roll(a, d, 0), pltpu.roll(u, d, 0), rows >= d
        u = jnp.where(ok, a * u_sh + u, u)
        a = jnp.where(ok, a * a_sh, a)
        d *= 2
    h = a * carry_sc[...] + u
    h_ref[...] = h
    carry_sc[...] = h[0:1] if reverse else h[tb - 1:tb]


def _rglru(rg_in, conv_w, conv_b, wg, bg, nsp, batch, seq, reverse, tb):
    m = batch * seq
    nb = seq // tb
    per8 = tb // SUBLANES
    last8 = m // SUBLANES - 1
    pos = (lambda j: nb - 1 - j) if reverse else (lambda j: j)
    blk = lambda b, j: b * nb + pos(j)
    kern = functools.partial(_rglru_kernel, tb=tb, nb=nb, reverse=reverse)
    return pl.pallas_call(
        kern,
        grid=(batch, nb),
        in_specs=[pl.BlockSpec((SUBLANES, RG_WIDTH), lambda b, j: (jnp.maximum(blk(b, j) * per8 - 1, 0), 0)),
                  pl.BlockSpec((tb, RG_WIDTH), lambda b, j: (blk(b, j), 0)),
                  pl.BlockSpec((SUBLANES, RG_WIDTH), lambda b, j: (jnp.minimum((blk(b, j) + 1) * per8, last8), 0)),
                  pl.BlockSpec(conv_w.shape, lambda b, j: (0, 0)),
                  pl.BlockSpec(conv_b.shape, lambda b, j: (0, 0)),
                  pl.BlockSpec(wg.shape, lambda b, j: (0, 0)),
                  pl.BlockSpec(bg.shape, lambda b, j: (0, 0)),
                  pl.BlockSpec(nsp.shape, lambda b, j: (0, 0))],
        out_specs=pl.BlockSpec((tb, RG_WIDTH), lambda b, j: (blk(b, j), 0)),
        out_shape=jax.ShapeDtypeStruct((m, RG_WIDTH), F32),
        scratch_shapes=[pltpu.VMEM((1, RG_WIDTH), F32)],
        compiler_params=_cparams(("parallel", "arbitrary")),
        name="rglru_bwd" if reverse else "rglru_fwd",
    )(rg_in, rg_in, rg_in, conv_w, conv_b, wg, bg, nsp)


def _mix_kernel(x_ref, of_ref, ob_ref, hgg_ref, ng_ref, hf_ref, hb_ref, ry_ref, da_ref, wout_ref,
                lg_ref, lb_ref, out_ref, *, alpha):
    o = of_ref[...] + ob_ref[...]
    bones = _head_block_ones(HG_WIDTH, HG_DK).astype(BF16)
    sq = o * o
    hi = sq.astype(BF16)
    lo = (sq - hi.astype(F32)).astype(BF16)
    ms = (jnp.dot(hi, bones, preferred_element_type=F32)
          + jnp.dot(lo, bones, preferred_element_type=F32)) * (1.0 / HG_DK)
    gate = hgg_ref[...]
    o_hg = o * lax.rsqrt(ms + RMS_EPS) * ng_ref[...] * (gate * jax.nn.sigmoid(gate))
    o_rg = (hf_ref[...] + hb_ref[...]) * _gelu_tanh(ry_ref[...])
    cat = jnp.concatenate([o_hg.astype(BF16), o_rg.astype(BF16), da_ref[...]], axis=1)
    mix = jnp.dot(cat, wout_ref[...], preferred_element_type=F32)
    out_ref[...] = _layer_norm(alpha * x_ref[...] + mix, lg_ref[...], lb_ref[...])


def _mix(x2, o_f, o_b, hg_in, norm_g, h_f, h_b, rg_in, o_da, w_out, ln_g, ln_b, alpha, tm):
    m = x2.shape[0]
    row = lambda w, cidx=0: pl.BlockSpec((tm, w), lambda i: (i, cidx))
    vec = lambda w: pl.BlockSpec((1, w), lambda i: (0, 0))
    return pl.pallas_call(
        functools.partial(_mix_kernel, alpha=alpha),
        grid=(m // tm,),
        in_specs=[row(D_MODEL), row(HG_WIDTH), row(HG_WIDTH), row(HG_WIDTH, 4), vec(HG_WIDTH),
                  row(RG_WIDTH), row(RG_WIDTH), row(RG_WIDTH, 1), row(DA_WIDTH),
                  _const_spec(w_out.shape), vec(D_MODEL), vec(D_MODEL)],
        out_specs=row(D_MODEL),
        out_shape=jax.ShapeDtypeStruct((m, D_MODEL), F32),
        compiler_params=_cparams(("parallel",)),
        name="mix_out",
    )(x2, o_f, o_b, hg_in, norm_g, h_f, h_b, rg_in, o_da, w_out, ln_g, ln_b)


def _matmul_kernel(a_ref, w_ref, o_ref):
    o_ref[...] = jnp.dot(a_ref[...].astype(BF16), w_ref[...], preferred_element_type=F32).astype(o_ref.dtype)


def _mem_proj(mem2, w):
    m, n = mem2.shape[0], w.shape[1]
    return pl.pallas_call(
        _matmul_kernel,
        grid=(1,),
        in_specs=[pl.BlockSpec(mem2.shape, lambda i: (0, 0)), pl.BlockSpec(w.shape, lambda i: (0, 0))],
        out_specs=pl.BlockSpec((m, n), lambda i: (0, 0)),
        out_shape=jax.ShapeDtypeStruct((m, n), BF16),
        compiler_params=_cparams(("arbitrary",)),
        name="mem_proj",
    )(mem2, w)


def _ca_kernel(x_ref, wq_ref, k_ref, v_ref, wo_ref, lg_ref, lb_ref, out_ref, *, alpha):
    x = x_ref[...]
    q = jnp.dot(x.astype(BF16), wq_ref[...], preferred_element_type=F32) * (CA_HEAD_DIM ** -0.5)
    outs = []
    for h in range(N_HEADS):
        sl = slice(h * CA_HEAD_DIM, (h + 1) * CA_HEAD_DIM)
        s = lax.dot_general(q[:, sl].astype(BF16), k_ref[:, sl], NT_DIMS, preferred_element_type=F32)
        p = jnp.exp(s - jnp.max(s, axis=1, keepdims=True))
        l = jnp.sum(p, axis=1, keepdims=True)
        o = jnp.dot(p.astype(BF16), v_ref[:, sl], preferred_element_type=F32) / l
        outs.append(o.astype(BF16))
    ca = jnp.dot(jnp.concatenate(outs, axis=1), wo_ref[...], preferred_element_type=F32)
    out_ref[...] = _layer_norm(alpha * x + ca, lg_ref[...], lb_ref[...])


def _cross_attention(x2, wq, kmem, vmem, wo, ln_g, ln_b, batch, seq, n_mem, alpha, tm):
    m = x2.shape[0]
    nt = seq // tm
    vec = pl.BlockSpec((1, D_MODEL), lambda b, i: (0, 0))
    return pl.pallas_call(
        functools.partial(_ca_kernel, alpha=alpha),
        grid=(batch, nt),
        in_specs=[pl.BlockSpec((tm, D_MODEL), lambda b, i: (b * nt + i, 0)),
                  _const_spec(wq.shape),
                  pl.BlockSpec((n_mem, D_MODEL), lambda b, i: (b, 0)),
                  pl.BlockSpec((n_mem, D_MODEL), lambda b, i: (b, 0)),
                  _const_spec(wo.shape), vec, vec],
        out_specs=pl.BlockSpec((tm, D_MODEL), lambda b, i: (b * nt + i, 0)),
        out_shape=jax.ShapeDtypeStruct((m, D_MODEL), F32),
        compiler_params=_cparams(("parallel", "parallel")),
        name="cross_attn",
    )(x2, wq, kmem, vmem, wo, ln_g, ln_b)


def _ffn_kernel(xp_ref, x_ref, xn_ref, wg_ref, wv_ref, cw_ref, cb_ref, wd_ref, lg_ref, lb_ref, out_ref,
                *, tm, nt, alpha):
    pos = pl.program_id(1)
    x = x_ref[...]
    xp = jnp.where(pos > 0, xp_ref[...], 0.0)
    xn = jnp.where(pos < nt - 1, xn_ref[...], 0.0)
    xe = jnp.concatenate([xp, x, xn], axis=0).astype(BF16)
    gate = jnp.dot(xe, wg_ref[...], preferred_element_type=F32)
    cw = cw_ref[...]
    o = SUBLANES - 1
    gc = (cw[0:1] * gate[o:o + tm] + cw[1:2] * gate[o + 1:o + 1 + tm] + cw[2:3] * gate[o + 2:o + 2 + tm]
          + cb_ref[...])
    val = jnp.dot(x.astype(BF16), wv_ref[...], preferred_element_type=F32)
    hid = (_gelu_tanh(gc) * val).astype(BF16)
    ff = jnp.dot(hid, wd_ref[...], preferred_element_type=F32)
    out_ref[...] = _layer_norm(alpha * x + ff, lg_ref[...], lb_ref[...])


def _ffn(x2, wg, wv, conv_w, conv_b, wd, ln_g, ln_b, batch, seq, alpha, tm):
    m = x2.shape[0]
    nt = seq // tm
    per8 = tm // SUBLANES
    last8 = m // SUBLANES - 1
    blk = lambda b, i: b * nt + i
    vec = pl.BlockSpec((1, D_MODEL), lambda b, i: (0, 0))
    return pl.pallas_call(
        functools.partial(_ffn_kernel, tm=tm, nt=nt, alpha=alpha),
        grid=(batch, nt),
        in_specs=[pl.BlockSpec((SUBLANES, D_MODEL), lambda b, i: (jnp.maximum(blk(b, i) * per8 - 1, 0), 0)),
                  pl.BlockSpec((tm, D_MODEL), lambda b, i: (blk(b, i), 0)),
                  pl.BlockSpec((SUBLANES, D_MODEL), lambda b, i: (jnp.minimum((blk(b, i) + 1) * per8, last8), 0)),
                  _const_spec(wg.shape), _const_spec(wv.shape),
                  pl.BlockSpec(conv_w.shape, lambda b, i: (0, 0)),
                  pl.BlockSpec(conv_b.shape, lambda b, i: (0, 0)),
                  _const_spec(wd.shape), vec, vec],
        out_specs=pl.BlockSpec((tm, D_MODEL), lambda b, i: (blk(b, i), 0)),
        out_shape=jax.ShapeDtypeStruct((m, D_MODEL), F32),
        compiler_params=_cparams(("parallel", "parallel")),
        name="conv_glu",
    )(x2, x2, x2, wg, wv, conv_w, conv_b, wd, ln_g, ln_b)


def _block_diag(w):
    n, blk, _ = w.shape
    eye = jnp.eye(n, dtype=w.dtype)
    return jnp.einsum("nij,nm->nimj", w, eye).reshape(n * blk, n * blk)


def _pick(seq, pref):
    t = min(seq, pref)
    assert seq % t == 0, (seq, pref)
    return t


def kernel(x, mem, positions, w_in, hg_lower_bounds, hg_norm_g, rg_conv_w, rg_conv_b, rg_wa, rg_ba, rg_wx, rg_bx, rg_lambda, da_lambda, da_subln_g, w_out, ln1_g, ln1_b, ca_wq, ca_wk, ca_wv, ca_wo, ln2_g, ln2_b, ffn_w_up, ffn_conv_w, ffn_conv_b, ffn_w_down, ln3_g, ln3_b):
    batch, seq, _ = x.shape
    n_mem = mem.shape[1]
    depth = w_in.shape[0]
    alpha = (2 * depth) ** 0.25
    m = batch * seq

    tm_proj = _pick(seq, 512)
    tm_ffn = _pick(seq, 256)
    tq = _pick(seq, 256)
    tb_hg = _pick(seq, 128)
    tb_rg = _pick(seq, 256)

    inv_freq = 1.0 / (ROPE_THETA ** (jnp.arange(0, DA_QK_DIM, 2, dtype=F32) / DA_QK_DIM))
    ang = positions.astype(F32).reshape(m, 1) * inv_freq
    cos, sin = jnp.cos(ang), jnp.sin(ang)
    zero = jnp.zeros_like(sin)
    cos_t = jnp.tile(cos, (1, 4))
    s1_t = jnp.tile(jnp.concatenate([-sin, zero], axis=1), (1, 2))
    s2_t = jnp.tile(jnp.concatenate([zero, sin], axis=1), (1, 2))

    lb_soft = jax.nn.softmax(hg_lower_bounds.astype(F32), axis=0)
    lb_all = jnp.cumsum(lb_soft, axis=0) - lb_soft[0:1]

    x2 = x.reshape(m, D_MODEL)
    mem2 = mem.reshape(batch * n_mem, D_MODEL)
    row = lambda v: v.reshape(1, -1).astype(F32)

    for layer in range(depth):
        w = w_in[layer].astype(BF16)
        o0 = 5 * HG_WIDTH
        o1 = o0 + 2 * RG_WIDTH
        hg_in, rg_in, qt, kk, vt = _inproj(
            x2, w[:, :o0], w[:, o0:o1], w[:, o1:o1 + DA_WIDTH], w[:, o1 + DA_WIDTH:o1 + 2 * DA_WIDTH],
            w[:, o1 + 2 * DA_WIDTH:], cos_t, s1_t, s2_t, tm_proj)

        lam_init = 0.8 - 0.6 * math.exp(-0.3 * layer)
        lp = da_lambda[layer].astype(F32)
        lam = jnp.exp(jnp.sum(lp[0] * lp[1])) - jnp.exp(jnp.sum(lp[2] * lp[3])) + lam_init
        o_da = _diff_attention(jnp.full((1, LANES), lam, F32), qt, kk, vt, row(da_subln_g[layer]),
                               batch, seq, 1.0 - lam_init, tq, tm_proj)

        o_f = _hgrn(hg_in, row(lb_all[layer, 0]), batch, seq, False, tb_hg, 16)
        o_b = _hgrn(hg_in, row(lb_all[layer, 1]), batch, seq, True, tb_hg, 16)

        nsp = -RG_C * jax.nn.softplus(-rg_lambda[layer].astype(F32))
        h_dirs = []
        for d in range(2):
            wg = jnp.concatenate([_block_diag(rg_wa[layer, d]), _block_diag(rg_wx[layer, d])], axis=1).astype(BF16)
            bg = jnp.concatenate([rg_ba[layer, d], rg_bx[layer, d]]).reshape(1, -1).astype(F32)
            h_dirs.append(_rglru(rg_in, rg_conv_w[layer].astype(F32), row(rg_conv_b[layer]), wg, bg,
                                 row(nsp[d]), batch, seq, d == 1, tb_rg))

        x2 = _mix(x2, o_f, o_b, hg_in, row(hg_norm_g[layer]), h_dirs[0], h_dirs[1], rg_in, o_da,
                  w_out[layer].astype(BF16), row(ln1_g[layer]), row(ln1_b[layer]), alpha, tm_proj)

        kmem = _mem_proj(mem2, ca_wk[layer].astype(BF16))
        vmem = _mem_proj(mem2, ca_wv[layer].astype(BF16))
        x2 = _cross_attention(x2, ca_wq[layer].astype(BF16), kmem, vmem, ca_wo[layer].astype(BF16),
                              row(ln2_g[layer]), row(ln2_b[layer]), batch, seq, n_mem, alpha, tm_proj)

        w_up = ffn_w_up[layer].astype(BF16)
        x2 = _ffn(x2, w_up[:, :D_FF], w_up[:, D_FF:], ffn_conv_w[layer].astype(F32), row(ffn_conv_b[layer]),
                  ffn_w_down[layer].astype(BF16), row(ln3_g[layer]), row(ln3_b[layer]), batch, seq, alpha, tm_ffn)

    return x2.reshape(batch, seq, D_MODEL)
```

```python
import functools
import math

import jax
import jax.numpy as jnp
from jax import lax
from jax.experimental import pallas as pl
from jax.experimental.pallas import tpu as pltpu

F32 = jnp.float32
BF16 = jnp.bfloat16

D_MODEL = 1024
N_HEADS = 4
HG_WIDTH = 256
HG_DK = 64
RG_WIDTH = 256
RG_BLOCK = 64
RG_C = 8.0
DA_WIDTH = 512
DA_V_DIM = 128
DA_QK_DIM = 64
ROPE_THETA = 10000.0
CA_HEAD_DIM = 256
D_FF = 2816
LN_EPS = 1e-5
RMS_EPS = 1e-6

LANES = 128
SUBLANES = 8
VMEM_LIMIT = 56 * 1024 * 1024

NT_DIMS = (((1,), (1,)), ((), ()))
TN_DIMS = (((0,), (0,)), ((), ()))
NEG_BIG = -1e30
TINY = 1e-30
LOG2_E = math.log2(math.e)
ATTN_UNROLL = 16
BF16_ROWS = 16
VT_ROWS = DA_V_DIM + BF16_ROWS


def _cparams(sem, flags=None):
    return pltpu.CompilerParams(dimension_semantics=sem, vmem_limit_bytes=VMEM_LIMIT, flags=flags)


def _const_spec(shape):
    nd = len(shape)
    return pl.BlockSpec(shape, lambda *_: (0,) * nd, pipeline_mode=pl.Buffered(1))


def _layer_norm(y, g, b):
    mu = jnp.mean(y, axis=-1, keepdims=True)
    d = y - mu
    var = jnp.mean(d * d, axis=-1, keepdims=True)
    return d * lax.rsqrt(var + LN_EPS) * g + b


def _gelu_tanh(x):
    return 0.5 * x * (1.0 + jnp.tanh(math.sqrt(2.0 / math.pi) * (x + 0.044715 * (x * x * x))))


def _head_block_ones(n, blk):
    r = lax.broadcasted_iota(jnp.int32, (n, n), 0) // blk
    c = lax.broadcasted_iota(jnp.int32, (n, n), 1) // blk
    return r == c


def _inproj_kernel(x_ref, whg_ref, wrg_ref, wq_ref, wk_ref, wv_ref, cos_ref, s1_ref, s2_ref,
                   hg_ref, rg_ref, qm_ref, k_ref, v_ref):
    xb = x_ref[...].astype(BF16)
    hg_ref[...] = jnp.dot(xb, whg_ref[...], preferred_element_type=F32)
    rg_ref[...] = jnp.dot(xb, wrg_ref[...], preferred_element_type=F32)

    reps = DA_WIDTH // LANES
    c = jnp.concatenate([cos_ref[...]] * reps, axis=1)
    s1 = jnp.concatenate([s1_ref[...]] * reps, axis=1)
    s2 = jnp.concatenate([s2_ref[...]] * reps, axis=1)
    half = DA_QK_DIM // 2

    def rot(t):
        return t * c + pltpu.roll(t, DA_WIDTH - half, 1) * s1 + pltpu.roll(t, half, 1) * s2

    q = rot(jnp.dot(xb, wq_ref[...], preferred_element_type=F32)) * (DA_QK_DIM ** -0.5 * LOG2_E)
    qt = q.T
    sub = lax.broadcasted_iota(jnp.int32, qt.shape, 0)
    first = (sub % DA_V_DIM) < DA_QK_DIM
    qm_ref[0] = jnp.where(first, qt, 0.0).astype(BF16)
    qm_ref[1] = jnp.where(first, 0.0, qt).astype(BF16)
    k_ref[...] = rot(jnp.dot(xb, wk_ref[...], preferred_element_type=F32)).astype(BF16)
    vt = jnp.dot(xb, wv_ref[...], preferred_element_type=F32).T.astype(BF16)
    ones = jnp.ones((VT_ROWS - DA_V_DIM, vt.shape[1]), BF16)
    for h in range(N_HEADS):
        v_ref[0, h * VT_ROWS:h * VT_ROWS + DA_V_DIM, :] = vt[h * DA_V_DIM:(h + 1) * DA_V_DIM]
        v_ref[0, h * VT_ROWS + DA_V_DIM:(h + 1) * VT_ROWS, :] = ones


def _inproj(x2, whg, wrg, wq, wk, wv, cos_t, s1_t, s2_t, tm):
    m = x2.shape[0]
    row = lambda w: pl.BlockSpec((tm, w), lambda i: (i, 0))
    return pl.pallas_call(
        _inproj_kernel,
        grid=(m // tm,),
        in_specs=[row(D_MODEL), _const_spec(whg.shape), _const_spec(wrg.shape), _const_spec(wq.shape),
                  _const_spec(wk.shape), _const_spec(wv.shape), row(LANES), row(LANES), row(LANES)],
        out_specs=[row(5 * HG_WIDTH), row(2 * RG_WIDTH),
                   pl.BlockSpec((2, DA_WIDTH, tm), lambda i: (0, 0, i)), row(DA_WIDTH),
                   pl.BlockSpec((1, N_HEADS * VT_ROWS, tm), lambda i: (i, 0, 0))],
        out_shape=[jax.ShapeDtypeStruct((m, 5 * HG_WIDTH), F32),
                   jax.ShapeDtypeStruct((m, 2 * RG_WIDTH), F32),
                   jax.ShapeDtypeStruct((2, DA_WIDTH, m), BF16),
                   jax.ShapeDtypeStruct((m, DA_WIDTH), BF16),
                   jax.ShapeDtypeStruct((m // tm, N_HEADS * VT_ROWS, tm), BF16)],
        compiler_params=_cparams(("parallel",)),
        name="inproj",
    )(x2, whg, wrg, wq, wk, wv, cos_t, s1_t, s2_t)


def _attn_kernel(lam_ref, qt_ref, k_ref, vt_ref, g_ref, o_ref, s_sc, acc_sc, *, tq, tk, nk, unroll, out_scale):
    qt = jnp.concatenate([qt_ref[0], qt_ref[1]], axis=1)
    acc_sc[...] = jnp.zeros(acc_sc.shape, F32)

    def scores(idx, slot):
        kc = k_ref[pl.ds(pl.multiple_of(idx * tk, tk), tk), :]
        s_sc[slot] = jnp.dot(kc, qt, preferred_element_type=F32)

    def absorb(idx, slot, m_prev):
        s = s_sc[slot]
        m_next = jnp.maximum(m_prev, jnp.max(s, axis=0, keepdims=True))
        p = jnp.exp2(s - m_next).astype(BF16)
        pv = jnp.dot(vt_ref[idx], p, preferred_element_type=F32)
        acc_sc[...] = jnp.exp2(m_prev - m_next) * acc_sc[...] + pv
        return m_next

    scores(0, 0)

    def body(t, m_run):
        for u in range(unroll):
            c = unroll * t + u
            scores(jnp.minimum(c + 1, nk - 1), (u + 1) % 2)
            m_run = absorb(c, u % 2, m_run)
        return m_run

    lax.fori_loop(0, nk // unroll, body, jnp.full((1, 2 * tq), -jnp.inf, F32))
    o = acc_sc[:DA_V_DIM, :] / acc_sc[DA_V_DIM:DA_V_DIM + 1, :]
    d = o[:, :tq] - lam_ref[...][:, :1] * o[:, tq:]
    ms = jnp.mean(d * d, axis=0, keepdims=True)
    dn = (d * lax.rsqrt(ms + RMS_EPS)).T
    o_ref[...] = (dn * g_ref[...] * out_scale).astype(o_ref.dtype)


def _diff_attention(lam_row, qt, kk, vt, subln_g, batch, seq, out_scale, tq, tk):
    m = batch * seq
    nq = seq // tq
    nk = seq // tk
    unroll = math.gcd(nk, ATTN_UNROLL)
    assert unroll % 2 == 0, nk
    kern = functools.partial(_attn_kernel, tq=tq, tk=tk, nk=nk, unroll=unroll, out_scale=out_scale)
    return pl.pallas_call(
        kern,
        grid=(batch, N_HEADS, nq),
        in_specs=[pl.BlockSpec((1, LANES), lambda b, h, i: (0, 0)),
                  pl.BlockSpec((2, DA_V_DIM, tq), lambda b, h, i: (0, h, b * nq + i)),
                  pl.BlockSpec((seq, DA_V_DIM), lambda b, h, i: (b, h)),
                  pl.BlockSpec((nk, VT_ROWS, tk), lambda b, h, i: (b, h, 0)),
                  pl.BlockSpec((1, DA_V_DIM), lambda b, h, i: (0, 0))],
        out_specs=pl.BlockSpec((tq, DA_V_DIM), lambda b, h, i: (b * nq + i, h)),
        out_shape=jax.ShapeDtypeStruct((m, DA_WIDTH), BF16),
        scratch_shapes=[pltpu.VMEM((2, tk, 2 * tq), F32), pltpu.VMEM((VT_ROWS, 2 * tq), F32)],
        compiler_params=_cparams(("parallel", "parallel", "arbitrary")),
        name="diff_attn",
    )(lam_row, qt, kk, vt, subln_g)


def _hgrn_kernel(q_ref, f_ref, v_ref, lb_ref, o_ref, st_sc, *, tb, chunk, reverse):
    @pl.when(pl.program_id(1) == 0)
    def _():
        st_sc[...] = jnp.zeros(st_sc.shape, F32)

    lb = lb_ref[...]
    z = f_ref[...]
    qraw = q_ref[...]
    qs = qraw * jax.nn.sigmoid(qraw)
    g = jnp.log(lb + (1.0 - lb) * jax.nn.sigmoid(z))
    kfull = (1.0 - lb) * jax.nn.sigmoid(-z)
    vfull = v_ref[...]

    r = lax.broadcasted_iota(jnp.int32, (tb, tb), 0)
    c = lax.broadcasted_iota(jnp.int32, (tb, tb), 1)
    tri = ((r // chunk) == (c // chunk)) & ((c >= r) if reverse else (c <= r))
    b = jnp.dot(tri.astype(F32), g, precision=lax.Precision.HIGHEST, preferred_element_type=F32)

    bones = _head_block_ones(HG_WIDTH, HG_DK).astype(BF16)
    pair_head = _head_block_ones(LANES, HG_DK)
    half = chunk // 2
    rows = lax.broadcasted_iota(jnp.int32, (half, HG_WIDTH), 0)
    zeros_half = jnp.zeros((half, HG_WIDTH), F32)

    order = range(tb // chunk - 1, -1, -1) if reverse else range(tb // chunk)
    for ci in order:
        sl = slice(ci * chunk, (ci + 1) * chunk)
        bc, qc, kc, vc = b[sl], qs[sl], kfull[sl], vfull[sl]
        b_end = bc[0:1] if reverse else bc[chunk - 1:chunk]
        qt = (qc * jnp.exp(bc)).astype(BF16)
        kh = (kc * jnp.exp(b_end - bc)).astype(BF16)
        vb = vc.astype(BF16)
        dec = jnp.exp(b_end)
        inter = []
        for p in range(HG_WIDTH // LANES):
            ln = slice(p * LANES, (p + 1) * LANES)
            st = st_sc[p]
            inter.append(lax.dot_general(qt[:, ln], st.astype(BF16), NT_DIMS, preferred_element_type=F32))
            kv = lax.dot_general(vb[:, ln], kh[:, ln], TN_DIMS, preferred_element_type=F32)
            st_sc[p] = jnp.where(pair_head, st * dec[:, ln] + kv, 0.0)
        o_c = jnp.concatenate(inter, axis=1)
        pieces = []
        for s in range(chunk):
            parts = []
            for hf in range(2):
                lo = hf * half
                live = (lo + half - 1 >= s) if not reverse else (lo <= s)
                if not live:
                    parts.append(zeros_half)
                    continue
                keep = (rows + lo <= s) if reverse else (rows + lo >= s)
                e = jnp.exp(jnp.where(keep, bc[lo:lo + half] - bc[s:s + 1], NEG_BIG))
                parts.append(qc[lo:lo + half] * e * kc[s:s + 1])
            pieces.append(jnp.concatenate(parts, axis=0).astype(BF16))
        att = jnp.dot(jnp.concatenate(pieces, axis=0), bones, preferred_element_type=F32)
        for s in range(chunk):
            o_c = o_c + att[s * chunk:(s + 1) * chunk] * vc[s:s + 1]
        o_ref[sl, :] = o_c


def _hgrn(hg_in, lb_row, batch, seq, reverse, tb, chunk):
    m = batch * seq
    nb = seq // tb
    pos = (lambda j: nb - 1 - j) if reverse else (lambda j: j)
    col = lambda cidx: pl.BlockSpec((tb, HG_WIDTH), lambda b, j: (b * nb + pos(j), cidx))
    kern = functools.partial(_hgrn_kernel, tb=tb, chunk=chunk, reverse=reverse)
    return pl.pallas_call(
        kern,
        grid=(batch, nb),
        in_specs=[col(0), col(2 if reverse else 1), col(3), pl.BlockSpec((1, HG_WIDTH), lambda b, j: (0, 0))],
        out_specs=pl.BlockSpec((tb, HG_WIDTH), lambda b, j: (b * nb + pos(j), 0)),
        out_shape=jax.ShapeDtypeStruct((m, HG_WIDTH), F32),
        scratch_shapes=[pltpu.VMEM((HG_WIDTH // LANES, LANES, LANES), F32)],
        compiler_params=_cparams(("parallel", "arbitrary")),
        name="hgrn_bwd" if reverse else "hgrn_fwd",
    )(hg_in, hg_in, hg_in, lb_row)


def _rglru_kernel(xp_ref, x_ref, xn_ref, cw_ref, cb_ref, wg_ref, bg_ref, nsp_ref, h_ref, carry_sc,
                  *, tb, nb, reverse):
    j = pl.program_id(1)
    pos = (nb - 1 - j) if reverse else j

    @pl.when(j == 0)
    def _():
        carry_sc[...] = jnp.zeros(carry_sc.shape, F32)

    xp = jnp.where(pos > 0, xp_ref[...], 0.0)
    xn = jnp.where(pos < nb - 1, xn_ref[...], 0.0)
    xe = jnp.concatenate([xp, x_ref[...], xn], axis=0)
    cw = cw_ref[...]
    ne = tb + 2 * SUBLANES
    mid = slice(SUBLANES, SUBLANES + tb)
    tap = lambda k: pltpu.roll(xe, (-k) % ne, 0)[mid]
    xc = cw[0:1] * tap(-2) + cw[1:2] * tap(-1) + cw[2:3] * xe[mid] + cw[3:4] * tap(1) + cb_ref[...]
    gates = jnp.dot(xc.astype(BF16), wg_ref[...], preferred_element_type=F32) + bg_ref[...]
    rgate = jax.nn.sigmoid(gates[:, :RG_WIDTH])
    igate = jax.nn.sigmoid(gates[:, RG_WIDTH:])
    log_a = nsp_ref[...] * rgate
    a = jnp.exp(log_a)
    one_m_a2 = 1.0 - a * a
    u = one_m_a2 * lax.rsqrt(jnp.maximum(one_m_a2, TINY)) * (igate * xc)

    rows = lax.broadcasted_iota(jnp.int32, (tb, RG_WIDTH), 0) % SUBLANES
    d = 1
    while d < SUBLANES:
        if reverse:
            a_sh, u_sh, ok = pltpu.roll(a, tb - d, 0), pltpu.roll(u, tb - d, 0), rows < SUBLANES - d
        else:
            a_sh, u_sh, ok = pltpu.roll(a, d, 0), pltpu.roll(u, d, 0), rows >= d
        u = jnp.where(ok, a * u_sh + u, u)
        a = jnp.where(ok, a * a_sh, a)
        d *= 2
    ngroups = tb // SUBLANES
    carry = carry_sc[...]
    out = [None] * ngroups
    for gi in (range(ngroups - 1, -1, -1) if reverse else range(ngroups)):
        sl = slice(gi * SUBLANES, (gi + 1) * SUBLANES)
        hg = a[sl] * carry + u[sl]
        out[gi] = hg
        carry = hg[0:1] if reverse else hg[SUBLANES - 1:SUBLANES]
    h_ref[...] = jnp.concatenate(out, axis=0)
    carry_sc[...] = carry


def _rglru(rg_in, conv_w, conv_b, wg, bg, nsp, batch, seq, reverse, tb):
    m = batch * seq
    nb = seq // tb
    per8 = tb // SUBLANES
    last8 = m // SUBLANES - 1
    pos = (lambda j: nb - 1 - j) if reverse else (lambda j: j)
    blk = lambda b, j: b * nb + pos(j)
    kern = functools.partial(_rglru_kernel, tb=tb, nb=nb, reverse=reverse)
    return pl.pallas_call(
        kern,
        grid=(batch, nb),
        in_specs=[pl.BlockSpec((SUBLANES, RG_WIDTH), lambda b, j: (jnp.maximum(blk(b, j) * per8 - 1, 0), 0)),
                  pl.BlockSpec((tb, RG_WIDTH), lambda b, j: (blk(b, j), 0)),
                  pl.BlockSpec((SUBLANES, RG_WIDTH), lambda b, j: (jnp.minimum((blk(b, j) + 1) * per8, last8), 0)),
                  pl.BlockSpec(conv_w.shape, lambda b, j: (0, 0)),
                  pl.BlockSpec(conv_b.shape, lambda b, j: (0, 0)),
                  pl.BlockSpec(wg.shape, lambda b, j: (0, 0)),
                  pl.BlockSpec(bg.shape, lambda b, j: (0, 0)),
                  pl.BlockSpec(nsp.shape, lambda b, j: (0, 0))],
        out_specs=pl.BlockSpec((tb, RG_WIDTH), lambda b, j: (blk(b, j), 0)),
        out_shape=jax.ShapeDtypeStruct((m, RG_WIDTH), F32),
        scratch_shapes=[pltpu.VMEM((1, RG_WIDTH), F32)],
        compiler_params=_cparams(("parallel", "arbitrary")),
        name="rglru_bwd" if reverse else "rglru_fwd",
    )(rg_in, rg_in, rg_in, conv_w, conv_b, wg, bg, nsp)


def _mix_ca_kernel(x_ref, of_ref, ob_ref, hgg_ref, ng_ref, hf_ref, hb_ref, ry_ref, da_ref, wout_ref,
                   l1g_ref, l1b_ref, wq_ref, k_ref, v_ref, wo_ref, l2g_ref, l2b_ref, out_ref, *, alpha):
    o = of_ref[...] + ob_ref[...]
    bones = _head_block_ones(HG_WIDTH, HG_DK).astype(BF16)
    sq = o * o
    hi = sq.astype(BF16)
    lo = (sq - hi.astype(F32)).astype(BF16)
    ms = (jnp.dot(hi, bones, preferred_element_type=F32)
          + jnp.dot(lo, bones, preferred_element_type=F32)) * (1.0 / HG_DK)
    gate = hgg_ref[...]
    o_hg = o * lax.rsqrt(ms + RMS_EPS) * ng_ref[...] * (gate * jax.nn.sigmoid(gate))
    o_rg = (hf_ref[...] + hb_ref[...]) * _gelu_tanh(ry_ref[...])
    cat = jnp.concatenate([o_hg.astype(BF16), o_rg.astype(BF16), da_ref[...]], axis=1)
    mix = jnp.dot(cat, wout_ref[...], preferred_element_type=F32)
    x = _layer_norm(alpha * x_ref[...] + mix, l1g_ref[...], l1b_ref[...])

    q = jnp.dot(x.astype(BF16), wq_ref[...], preferred_element_type=F32) * (CA_HEAD_DIM ** -0.5)
    outs = []
    for h in range(N_HEADS):
        sl = slice(h * CA_HEAD_DIM, (h + 1) * CA_HEAD_DIM)
        s = lax.dot_general(q[:, sl].astype(BF16), k_ref[:, sl], NT_DIMS, preferred_element_type=F32)
        p = jnp.exp(s - jnp.max(s, axis=1, keepdims=True))
        l = jnp.sum(p, axis=1, keepdims=True)
        o = jnp.dot(p.astype(BF16), v_ref[:, sl], preferred_element_type=F32) / l
        outs.append(o.astype(BF16))
    ca = jnp.dot(jnp.concatenate(outs, axis=1), wo_ref[...], preferred_element_type=F32)
    out_ref[...] = _layer_norm(alpha * x + ca, l2g_ref[...], l2b_ref[...])


def _mix_ca(x2, o_f, o_b, hg_in, norm_g, h_f, h_b, rg_in, o_da, w_out, ln1_g, ln1_b,
            wq, kmem, vmem, wo, ln2_g, ln2_b, batch, seq, n_mem, alpha, tm):
    m = x2.shape[0]
    nt = seq // tm
    row = lambda w, cidx=0: pl.BlockSpec((tm, w), lambda b, i: (b * nt + i, cidx))
    vec = lambda w: pl.BlockSpec((1, w), lambda b, i: (0, 0))
    mem_blk = pl.BlockSpec((n_mem, D_MODEL), lambda b, i: (b, 0))
    return pl.pallas_call(
        functools.partial(_mix_ca_kernel, alpha=alpha),
        grid=(batch, nt),
        in_specs=[row(D_MODEL), row(HG_WIDTH), row(HG_WIDTH), row(HG_WIDTH, 4), vec(HG_WIDTH),
                  row(RG_WIDTH), row(RG_WIDTH), row(RG_WIDTH, 1), row(DA_WIDTH),
                  _const_spec(w_out.shape), vec(D_MODEL), vec(D_MODEL),
                  _const_spec(wq.shape), mem_blk, mem_blk, _const_spec(wo.shape), vec(D_MODEL), vec(D_MODEL)],
        out_specs=row(D_MODEL),
        out_shape=jax.ShapeDtypeStruct((m, D_MODEL), F32),
        compiler_params=_cparams(("parallel", "parallel")),
        name="mix_cross_attn",
    )(x2, o_f, o_b, hg_in, norm_g, h_f, h_b, rg_in, o_da, w_out, ln1_g, ln1_b,
      wq, kmem, vmem, wo, ln2_g, ln2_b)


def _matmul_kernel(a_ref, w_ref, o_ref):
    o_ref[...] = jnp.dot(a_ref[...].astype(BF16), w_ref[...], preferred_element_type=F32).astype(o_ref.dtype)


def _mem_proj(mem2, w):
    m, n = mem2.shape[0], w.shape[1]
    return pl.pallas_call(
        _matmul_kernel,
        grid=(1,),
        in_specs=[pl.BlockSpec(mem2.shape, lambda i: (0, 0)), pl.BlockSpec(w.shape, lambda i: (0, 0))],
        out_specs=pl.BlockSpec((m, n), lambda i: (0, 0)),
        out_shape=jax.ShapeDtypeStruct((m, n), BF16),
        compiler_params=_cparams(("arbitrary",)),
        name="mem_proj",
    )(mem2, w)


def _ffn_kernel(xp_ref, x_ref, xn_ref, wg_ref, wv_ref, cw_ref, cb_ref, wd_ref, lg_ref, lb_ref, out_ref,
                *, tm, nt, alpha):
    pos = pl.program_id(1)
    x = x_ref[...]
    xp = jnp.where(pos > 0, xp_ref[...], 0.0)
    xn = jnp.where(pos < nt - 1, xn_ref[...], 0.0)
    xe = jnp.concatenate([xp, x, xn], axis=0).astype(BF16)
    gate = jnp.dot(xe, wg_ref[...], preferred_element_type=F32)
    cw = cw_ref[...]
    ne = tm + 2 * SUBLANES
    mid = slice(SUBLANES, SUBLANES + tm)
    gc = (cw[0:1] * pltpu.roll(gate, 1, 0)[mid] + cw[1:2] * gate[mid]
          + cw[2:3] * pltpu.roll(gate, ne - 1, 0)[mid] + cb_ref[...])
    val = jnp.dot(x.astype(BF16), wv_ref[...], preferred_element_type=F32)
    hid = (_gelu_tanh(gc) * val).astype(BF16)
    ff = jnp.dot(hid, wd_ref[...], preferred_element_type=F32)
    out_ref[...] = _layer_norm(alpha * x + ff, lg_ref[...], lb_ref[...])


def _ffn(x2, wg, wv, conv_w, conv_b, wd, ln_g, ln_b, batch, seq, alpha, tm):
    m = x2.shape[0]
    nt = seq // tm
    per8 = tm // SUBLANES
    last8 = m // SUBLANES - 1
    blk = lambda b, i: b * nt + i
    vec = pl.BlockSpec((1, D_MODEL), lambda b, i: (0, 0))
    return pl.pallas_call(
        functools.partial(_ffn_kernel, tm=tm, nt=nt, alpha=alpha),
        grid=(batch, nt),
        in_specs=[pl.BlockSpec((SUBLANES, D_MODEL), lambda b, i: (jnp.maximum(blk(b, i) * per8 - 1, 0), 0)),
                  pl.BlockSpec((tm, D_MODEL), lambda b, i: (blk(b, i), 0)),
                  pl.BlockSpec((SUBLANES, D_MODEL), lambda b, i: (jnp.minimum((blk(b, i) + 1) * per8, last8), 0)),
                  _const_spec(wg.shape), _const_spec(wv.shape),
                  pl.BlockSpec(conv_w.shape, lambda b, i: (0, 0)),
                  pl.BlockSpec(conv_b.shape, lambda b, i: (0, 0)),
                  _const_spec(wd.shape), vec, vec],
        out_specs=pl.BlockSpec((tm, D_MODEL), lambda b, i: (blk(b, i), 0)),
        out_shape=jax.ShapeDtypeStruct((m, D_MODEL), F32),
        compiler_params=_cparams(("parallel", "parallel")),
        name="conv_glu",
    )(x2, x2, x2, wg, wv, conv_w, conv_b, wd, ln_g, ln_b)


def _block_diag(w):
    n, blk, _ = w.shape
    eye = jnp.eye(n, dtype=w.dtype)
    return jnp.einsum("nij,nm->nimj", w, eye).reshape(n * blk, n * blk)


def _pick(seq, pref):
    t = min(seq, pref)
    assert seq % t == 0, (seq, pref)
    return t


def kernel(x, mem, positions, w_in, hg_lower_bounds, hg_norm_g, rg_conv_w, rg_conv_b, rg_wa, rg_ba, rg_wx, rg_bx, rg_lambda, da_lambda, da_subln_g, w_out, ln1_g, ln1_b, ca_wq, ca_wk, ca_wv, ca_wo, ln2_g, ln2_b, ffn_w_up, ffn_conv_w, ffn_conv_b, ffn_w_down, ln3_g, ln3_b):
    batch, seq, _ = x.shape
    n_mem = mem.shape[1]
    depth = w_in.shape[0]
    alpha = (2 * depth) ** 0.25
    m = batch * seq

    tm_proj = _pick(seq, 512)
    tm_ffn = _pick(seq, 256)
    tq = _pick(seq, 256)
    tb_hg = _pick(seq, 128)
    tb_rg = _pick(seq, 256)

    inv_freq = 1.0 / (ROPE_THETA ** (jnp.arange(0, DA_QK_DIM, 2, dtype=F32) / DA_QK_DIM))
    ang = positions.astype(F32).reshape(m, 1) * inv_freq
    cos, sin = jnp.cos(ang), jnp.sin(ang)
    zero = jnp.zeros_like(sin)
    cos_t = jnp.tile(cos, (1, 4))
    s1_t = jnp.tile(jnp.concatenate([-sin, zero], axis=1), (1, 2))
    s2_t = jnp.tile(jnp.concatenate([zero, sin], axis=1), (1, 2))

    lb_soft = jax.nn.softmax(hg_lower_bounds.astype(F32), axis=0)
    lb_all = jnp.cumsum(lb_soft, axis=0) - lb_soft[0:1]

    x2 = x.reshape(m, D_MODEL)
    mem2 = mem.reshape(batch * n_mem, D_MODEL)
    row = lambda v: v.reshape(1, -1).astype(F32)

    for layer in range(depth):
        w = w_in[layer].astype(BF16)
        o0 = 5 * HG_WIDTH
        o1 = o0 + 2 * RG_WIDTH
        hg_in, rg_in, qt, kk, vt = _inproj(
            x2, w[:, :o0], w[:, o0:o1], w[:, o1:o1 + DA_WIDTH], w[:, o1 + DA_WIDTH:o1 + 2 * DA_WIDTH],
            w[:, o1 + 2 * DA_WIDTH:], cos_t, s1_t, s2_t, tm_proj)

        lam_init = 0.8 - 0.6 * math.exp(-0.3 * layer)
        lp = da_lambda[layer].astype(F32)
        lam = jnp.exp(jnp.sum(lp[0] * lp[1])) - jnp.exp(jnp.sum(lp[2] * lp[3])) + lam_init
        o_da = _diff_attention(jnp.full((1, LANES), lam, F32), qt, kk, vt, row(da_subln_g[layer]),
                               batch, seq, 1.0 - lam_init, tq, tm_proj)

        o_f = _hgrn(hg_in, row(lb_all[layer, 0]), batch, seq, False, tb_hg, 16)
        o_b = _hgrn(hg_in, row(lb_all[layer, 1]), batch, seq, True, tb_hg, 16)

        nsp = -RG_C * jax.nn.softplus(-rg_lambda[layer].astype(F32))
        h_dirs = []
        for d in range(2):
            wg = jnp.concatenate([_block_diag(rg_wa[layer, d]), _block_diag(rg_wx[layer, d])], axis=1).astype(BF16)
            bg = jnp.concatenate([rg_ba[layer, d], rg_bx[layer, d]]).reshape(1, -1).astype(F32)
            h_dirs.append(_rglru(rg_in, rg_conv_w[layer].astype(F32), row(rg_conv_b[layer]), wg, bg,
                                 row(nsp[d]), batch, seq, d == 1, tb_rg))

        kmem = _mem_proj(mem2, ca_wk[layer].astype(BF16))
        vmem = _mem_proj(mem2, ca_wv[layer].astype(BF16))
        x2 = _mix_ca(x2, o_f, o_b, hg_in, row(hg_norm_g[layer]), h_dirs[0], h_dirs[1], rg_in, o_da,
                     w_out[layer].astype(BF16), row(ln1_g[layer]), row(ln1_b[layer]),
                     ca_wq[layer].astype(BF16), kmem, vmem, ca_wo[layer].astype(BF16),
                     row(ln2_g[layer]), row(ln2_b[layer]), batch, seq, n_mem, alpha, tm_proj)

        w_up = ffn_w_up[layer].astype(BF16)
        x2 = _ffn(x2, w_up[:, :D_FF], w_up[:, D_FF:], ffn_conv_w[layer].astype(F32), row(ffn_conv_b[layer]),
                  ffn_w_down[layer].astype(BF16), row(ln3_g[layer]), row(ln3_b[layer]), batch, seq, alpha, tm_ffn)

    return x2.reshape(batch, seq, D_MODEL)
```

```python
import functools
import math

import jax
import jax.numpy as jnp
from jax import lax
from jax.experimental import pallas as pl
from jax.experimental.pallas import tpu as pltpu

F32 = jnp.float32
BF16 = jnp.bfloat16

D_MODEL = 1024
N_HEADS = 4
HG_WIDTH = 256
HG_DK = 64
RG_WIDTH = 256
RG_BLOCK = 64
RG_C = 8.0
DA_WIDTH = 512
DA_V_DIM = 128
DA_QK_DIM = 64
ROPE_THETA = 10000.0
CA_HEAD_DIM = 256
D_FF = 2816
LN_EPS = 1e-5
RMS_EPS = 1e-6

LANES = 128
SUBLANES = 8
VMEM_LIMIT = 56 * 1024 * 1024

NT_DIMS = (((1,), (1,)), ((), ()))
TN_DIMS = (((0,), (0,)), ((), ()))
NEG_BIG = -1e30
TINY = 1e-30
LOG2_E = math.log2(math.e)
ATTN_UNROLL = 16
BF16_ROWS = 16
VT_ROWS = DA_V_DIM + BF16_ROWS


def _cparams(sem, flags=None):
    return pltpu.CompilerParams(dimension_semantics=sem, vmem_limit_bytes=VMEM_LIMIT, flags=flags)


def _const_spec(shape):
    nd = len(shape)
    return pl.BlockSpec(shape, lambda *_: (0,) * nd, pipeline_mode=pl.Buffered(1))


def _layer_norm(y, g, b):
    mu = jnp.mean(y, axis=-1, keepdims=True)
    d = y - mu
    var = jnp.mean(d * d, axis=-1, keepdims=True)
    return d * lax.rsqrt(var + LN_EPS) * g + b


def _gelu_tanh(x):
    return 0.5 * x * (1.0 + jnp.tanh(math.sqrt(2.0 / math.pi) * (x + 0.044715 * (x * x * x))))


def _head_block_ones(n, blk):
    r = lax.broadcasted_iota(jnp.int32, (n, n), 0) // blk
    c = lax.broadcasted_iota(jnp.int32, (n, n), 1) // blk
    return r == c


def _inproj_kernel(x_ref, w_ref, cos_ref, sin_ref, hg_ref, rg_ref, qm_ref, k_ref, v_ref):
    xb = x_ref[...].astype(BF16)
    o_rg = 5 * HG_WIDTH
    o_q = o_rg + 2 * RG_WIDTH
    o_k, o_v = o_q + DA_WIDTH, o_q + 2 * DA_WIDTH
    proj = lambda lo, hi: jnp.dot(xb, w_ref[:, lo:hi], preferred_element_type=F32)
    hg_ref[...] = proj(0, o_rg)
    rg_ref[...] = proj(o_rg, o_q)

    reps = DA_WIDTH // LANES
    c = jnp.concatenate([cos_ref[...]] * reps, axis=1)
    sgn_sin = jnp.concatenate([sin_ref[...]] * reps, axis=1)
    half = DA_QK_DIM // 2
    lane = lax.broadcasted_iota(jnp.int32, c.shape, 1)
    first_half = (lane % DA_QK_DIM) < half

    def rot(t):
        partner = jnp.where(first_half, pltpu.roll(t, DA_WIDTH - half, 1), pltpu.roll(t, half, 1))
        return t * c + partner * sgn_sin

    q = rot(proj(o_q, o_k)) * (DA_QK_DIM ** -0.5 * LOG2_E)
    qt = q.T
    sub = lax.broadcasted_iota(jnp.int32, qt.shape, 0)
    first = (sub % DA_V_DIM) < DA_QK_DIM
    qm_ref[0] = jnp.where(first, qt, 0.0).astype(BF16)
    qm_ref[1] = jnp.where(first, 0.0, qt).astype(BF16)
    k_ref[...] = rot(proj(o_k, o_v)).astype(BF16)
    vt = proj(o_v, o_v + DA_WIDTH).T.astype(BF16)
    ones = jnp.ones((VT_ROWS - DA_V_DIM, vt.shape[1]), BF16)
    for h in range(N_HEADS):
        v_ref[0, h * VT_ROWS:h * VT_ROWS + DA_V_DIM, :] = vt[h * DA_V_DIM:(h + 1) * DA_V_DIM]
        v_ref[0, h * VT_ROWS + DA_V_DIM:(h + 1) * VT_ROWS, :] = ones


def _inproj(x2, w, cos_t, sin_t, tm):
    m = x2.shape[0]
    row = lambda w: pl.BlockSpec((tm, w), lambda i: (i, 0))
    return pl.pallas_call(
        _inproj_kernel,
        grid=(m // tm,),
        in_specs=[row(D_MODEL), _const_spec(w.shape), row(LANES), row(LANES)],
        out_specs=[row(5 * HG_WIDTH), row(2 * RG_WIDTH),
                   pl.BlockSpec((2, DA_WIDTH, tm), lambda i: (0, 0, i)), row(DA_WIDTH),
                   pl.BlockSpec((1, N_HEADS * VT_ROWS, tm), lambda i: (i, 0, 0))],
        out_shape=[jax.ShapeDtypeStruct((m, 5 * HG_WIDTH), F32),
                   jax.ShapeDtypeStruct((m, 2 * RG_WIDTH), F32),
                   jax.ShapeDtypeStruct((2, DA_WIDTH, m), BF16),
                   jax.ShapeDtypeStruct((m, DA_WIDTH), BF16),
                   jax.ShapeDtypeStruct((m // tm, N_HEADS * VT_ROWS, tm), BF16)],
        compiler_params=_cparams(("parallel",)),
        name="inproj",
    )(x2, w, cos_t, sin_t)


def _attn_kernel(lam_ref, qt_ref, qn_ref, k_ref, vt_ref, g_ref, o_ref, q_sc, s_sc, acc_sc,
                 *, tq, tk, nk, unroll, out_scale):
    q_sc[0] = jnp.concatenate([qt_ref[0], qt_ref[1]], axis=1)
    q_sc[1] = jnp.concatenate([qn_ref[0], qn_ref[1]], axis=1)
    acc_sc[...] = jnp.zeros(acc_sc.shape, F32)

    def scores(idx, slot, tile):
        kc = k_ref[pl.ds(pl.multiple_of(idx * tk, tk), tk), :]
        s_sc[slot] = jnp.dot(kc, q_sc[tile], preferred_element_type=F32)

    def absorb(idx, slot, m_prev):
        s = s_sc[slot]
        m_next = jnp.maximum(m_prev, jnp.max(s, axis=0, keepdims=True))
        p = jnp.exp2(s - m_next).astype(BF16)
        pv = jnp.dot(vt_ref[idx], p, preferred_element_type=F32)
        acc_sc[...] = jnp.exp2(m_prev - m_next) * acc_sc[...] + pv
        return m_next

    @pl.when(pl.program_id(2) == 0)
    def _():
        scores(0, 0, 0)

    def body(t, m_run):
        for u in range(unroll):
            c = unroll * t + u
            wrap = (c + 1 == nk).astype(jnp.int32)
            scores((c + 1) * (1 - wrap), (u + 1) % 2, wrap)
            m_run = absorb(c, u % 2, m_run)
        return m_run

    lax.fori_loop(0, nk // unroll, body, jnp.full((1, 2 * tq), -jnp.inf, F32))
    o = acc_sc[:DA_V_DIM, :] / acc_sc[DA_V_DIM:DA_V_DIM + 1, :]
    d = o[:, :tq] - lam_ref[...][:, :1] * o[:, tq:]
    ms = jnp.mean(d * d, axis=0, keepdims=True)
    dn = (d * lax.rsqrt(ms + RMS_EPS)).T
    o_ref[...] = (dn * g_ref[...] * out_scale).astype(o_ref.dtype)


def _diff_attention(lam_row, qt, kk, vt, subln_g, batch, seq, out_scale, tq, tk):
    m = batch * seq
    nq = seq // tq
    nk = seq // tk
    unroll = math.gcd(nk, ATTN_UNROLL)
    assert unroll % 2 == 0, nk
    kern = functools.partial(_attn_kernel, tq=tq, tk=tk, nk=nk, unroll=unroll, out_scale=out_scale)
    return pl.pallas_call(
        kern,
        grid=(batch, N_HEADS, nq),
        in_specs=[pl.BlockSpec((1, LANES), lambda b, h, i: (0, 0)),
                  pl.BlockSpec((2, DA_V_DIM, tq), lambda b, h, i: (0, h, b * nq + i)),
                  pl.BlockSpec((2, DA_V_DIM, tq), lambda b, h, i: (0, h, b * nq + jnp.minimum(i + 1, nq - 1))),
                  pl.BlockSpec((seq, DA_V_DIM), lambda b, h, i: (b, h)),
                  pl.BlockSpec((nk, VT_ROWS, tk), lambda b, h, i: (b, h, 0)),
                  pl.BlockSpec((1, DA_V_DIM), lambda b, h, i: (0, 0))],
        out_specs=pl.BlockSpec((tq, DA_V_DIM), lambda b, h, i: (b * nq + i, h)),
        out_shape=jax.ShapeDtypeStruct((m, DA_WIDTH), BF16),
        scratch_shapes=[pltpu.VMEM((2, DA_V_DIM, 2 * tq), BF16), pltpu.VMEM((2, tk, 2 * tq), F32),
                        pltpu.VMEM((VT_ROWS, 2 * tq), F32)],
        compiler_params=_cparams(("parallel", "parallel", "arbitrary")),
        name="diff_attn",
    )(lam_row, qt, qt, kk, vt, subln_g)


def _hgrn_kernel(qf_ref, ff_ref, vf_ref, lbf_ref, qb_ref, fb_ref, vb_ref, lbb_ref, of_ref, ob_ref,
                 stf_sc, stb_sc, *, tb, chunk):
    @pl.when(pl.program_id(1) == 0)
    def _():
        stf_sc[...] = jnp.zeros(stf_sc.shape, F32)
        stb_sc[...] = jnp.zeros(stb_sc.shape, F32)

    _hgrn_direction(qf_ref, ff_ref, vf_ref, lbf_ref, of_ref, stf_sc, tb, chunk, False)
    _hgrn_direction(qb_ref, fb_ref, vb_ref, lbb_ref, ob_ref, stb_sc, tb, chunk, True)


def _hgrn_direction(q_ref, f_ref, v_ref, lb_ref, o_ref, st_sc, tb, chunk, reverse):
    lb = lb_ref[...]
    z = f_ref[...]
    qraw = q_ref[...]
    qs = qraw * jax.nn.sigmoid(qraw)
    g = jnp.log(lb + (1.0 - lb) * jax.nn.sigmoid(z))
    kfull = (1.0 - lb) * jax.nn.sigmoid(-z)
    vfull = v_ref[...]

    r = lax.broadcasted_iota(jnp.int32, (tb, tb), 0)
    c = lax.broadcasted_iota(jnp.int32, (tb, tb), 1)
    tri = ((r // chunk) == (c // chunk)) & ((c >= r) if reverse else (c <= r))
    b = jnp.dot(tri.astype(F32), g, precision=lax.Precision.HIGHEST, preferred_element_type=F32)

    bones = _head_block_ones(HG_WIDTH, HG_DK).astype(BF16)
    pair_head = _head_block_ones(LANES, HG_DK)
    half = chunk // 2
    rows = lax.broadcasted_iota(jnp.int32, (half, HG_WIDTH), 0)
    zeros_half = jnp.zeros((half, HG_WIDTH), F32)

    order = range(tb // chunk - 1, -1, -1) if reverse else range(tb // chunk)
    for ci in order:
        sl = slice(ci * chunk, (ci + 1) * chunk)
        bc, qc, kc, vc = b[sl], qs[sl], kfull[sl], vfull[sl]
        b_end = bc[0:1] if reverse else bc[chunk - 1:chunk]
        qt = (qc * jnp.exp(bc)).astype(BF16)
        kh = (kc * jnp.exp(b_end - bc)).astype(BF16)
        vb = vc.astype(BF16)
        dec = jnp.exp(b_end)
        inter = []
        for p in range(HG_WIDTH // LANES):
            ln = slice(p * LANES, (p + 1) * LANES)
            st = st_sc[p]
            inter.append(lax.dot_general(qt[:, ln], st.astype(BF16), NT_DIMS, preferred_element_type=F32))
            kv = lax.dot_general(vb[:, ln], kh[:, ln], TN_DIMS, preferred_element_type=F32)
            st_sc[p] = jnp.where(pair_head, st * dec[:, ln] + kv, 0.0)
        o_c = jnp.concatenate(inter, axis=1)
        pieces = []
        for s in range(chunk):
            parts = []
            for hf in range(2):
                lo = hf * half
                live = (lo + half - 1 >= s) if not reverse else (lo <= s)
                if not live:
                    parts.append(zeros_half)
                    continue
                keep = (rows + lo <= s) if reverse else (rows + lo >= s)
                e = jnp.exp(jnp.where(keep, bc[lo:lo + half] - bc[s:s + 1], NEG_BIG))
                parts.append(qc[lo:lo + half] * e * kc[s:s + 1])
            pieces.append(jnp.concatenate(parts, axis=0).astype(BF16))
        att = jnp.dot(jnp.concatenate(pieces, axis=0), bones, preferred_element_type=F32)
        for s in range(chunk):
            o_c = o_c + att[s * chunk:(s + 1) * chunk] * vc[s:s + 1]
        o_ref[sl, :] = o_c


def _hgrn(hg_in, lb_fwd, lb_bwd, batch, seq, tb, chunk):
    m = batch * seq
    nb = seq // tb
    fwd = lambda cidx: pl.BlockSpec((tb, HG_WIDTH), lambda b, j: (b * nb + j, cidx))
    bwd = lambda cidx: pl.BlockSpec((tb, HG_WIDTH), lambda b, j: (b * nb + nb - 1 - j, cidx))
    vec = pl.BlockSpec((1, HG_WIDTH), lambda b, j: (0, 0))
    state = pltpu.VMEM((HG_WIDTH // LANES, LANES, LANES), F32)
    return pl.pallas_call(
        functools.partial(_hgrn_kernel, tb=tb, chunk=chunk),
        grid=(batch, nb),
        in_specs=[fwd(0), fwd(1), fwd(3), vec, bwd(0), bwd(2), bwd(3), vec],
        out_specs=[fwd(0), bwd(0)],
        out_shape=[jax.ShapeDtypeStruct((m, HG_WIDTH), F32)] * 2,
        scratch_shapes=[state, state],
        compiler_params=_cparams(("parallel", "arbitrary")),
        name="hgrn",
    )(hg_in, hg_in, hg_in, lb_fwd, hg_in, hg_in, hg_in, lb_bwd)


def _rglru_kernel(xp_ref, x_ref, xn_ref, cw_ref, cb_ref, wg_ref, bg_ref, nsp_ref, h_ref, carry_sc,
                  *, tb, nb, reverse):
    j = pl.program_id(1)
    pos = (nb - 1 - j) if reverse else j

    @pl.when(j == 0)
    def _():
        carry_sc[...] = jnp.zeros(carry_sc.shape, F32)

    xp = jnp.where(pos > 0, xp_ref[...], 0.0)
    xn = jnp.where(pos < nb - 1, xn_ref[...], 0.0)
    xe = jnp.concatenate([xp, x_ref[...], xn], axis=0)
    cw = cw_ref[...]
    ne = tb + 2 * SUBLANES
    mid = slice(SUBLANES, SUBLANES + tb)
    tap = lambda k: pltpu.roll(xe, (-k) % ne, 0)[mid]
    xc = cw[0:1] * tap(-2) + cw[1:2] * tap(-1) + cw[2:3] * xe[mid] + cw[3:4] * tap(1) + cb_ref[...]
    gates = jnp.dot(xc.astype(BF16), wg_ref[...], preferred_element_type=F32) + bg_ref[...]
    rgate = jax.nn.sigmoid(gates[:, :RG_WIDTH])
    igate = jax.nn.sigmoid(gates[:, RG_WIDTH:])
    log_a = nsp_ref[...] * rgate
    a = jnp.exp(log_a)
    one_m_a2 = 1.0 - a * a
    u = one_m_a2 * lax.rsqrt(jnp.maximum(one_m_a2, TINY)) * (igate * xc)

    rows = lax.broadcasted_iota(jnp.int32, (tb, RG_WIDTH), 0) % SUBLANES
    d = 1
    while d < SUBLANES:
        if reverse:
            a_sh, u_sh, ok = pltpu.roll(a, tb - d, 0), pltpu.roll(u, tb - d, 0), rows < SUBLANES - d
        else:
            a_sh, u_sh, ok = pltpu.roll(a, d, 0), pltpu.roll(u, d, 0), rows >= d
        u = jnp.where(ok, a * u_sh + u, u)
        a = jnp.where(ok, a * a_sh, a)
        d *= 2
    ngroups = tb // SUBLANES
    carry = carry_sc[...]
    out = [None] * ngroups
    for gi in (range(ngroups - 1, -1, -1) if reverse else range(ngroups)):
        sl = slice(gi * SUBLANES, (gi + 1) * SUBLANES)
        hg = a[sl] * carry + u[sl]
        out[gi] = hg
        carry = hg[0:1] if reverse else hg[SUBLANES - 1:SUBLANES]
    h_ref[...] = jnp.concatenate(out, axis=0)
    carry_sc[...] = carry


def _rglru(rg_in, conv_w, conv_b, wg, bg, nsp, batch, seq, reverse, tb):
    m = batch * seq
    nb = seq // tb
    per8 = tb // SUBLANES
    last8 = m // SUBLANES - 1
    pos = (lambda j: nb - 1 - j) if reverse else (lambda j: j)
    blk = lambda b, j: b * nb + pos(j)
    kern = functools.partial(_rglru_kernel, tb=tb, nb=nb, reverse=reverse)
    return pl.pallas_call(
        kern,
        grid=(batch, nb),
        in_specs=[pl.BlockSpec((SUBLANES, RG_WIDTH), lambda b, j: (jnp.maximum(blk(b, j) * per8 - 1, 0), 0)),
                  pl.BlockSpec((tb, RG_WIDTH), lambda b, j: (blk(b, j), 0)),
                  pl.BlockSpec((SUBLANES, RG_WIDTH), lambda b, j: (jnp.minimum((blk(b, j) + 1) * per8, last8), 0)),
                  pl.BlockSpec(conv_w.shape, lambda b, j: (0, 0)),
                  pl.BlockSpec(conv_b.shape, lambda b, j: (0, 0)),
                  pl.BlockSpec(wg.shape, lambda b, j: (0, 0)),
                  pl.BlockSpec(bg.shape, lambda b, j: (0, 0)),
                  pl.BlockSpec(nsp.shape, lambda b, j: (0, 0))],
        out_specs=pl.BlockSpec((tb, RG_WIDTH), lambda b, j: (blk(b, j), 0)),
        out_shape=jax.ShapeDtypeStruct((m, RG_WIDTH), F32),
        scratch_shapes=[pltpu.VMEM((1, RG_WIDTH), F32)],
        compiler_params=_cparams(("parallel", "arbitrary")),
        name="rglru_bwd" if reverse else "rglru_fwd",
    )(rg_in, rg_in, rg_in, conv_w, conv_b, wg, bg, nsp)


def _mix_ca_kernel(x_ref, of_ref, ob_ref, hgg_ref, ng_ref, hf_ref, hb_ref, ry_ref, da_ref, wout_ref,
                   l1g_ref, l1b_ref, wq_ref, k_ref, v_ref, wo_ref, l2g_ref, l2b_ref, out_ref, *, alpha):
    o = of_ref[...] + ob_ref[...]
    bones = _head_block_ones(HG_WIDTH, HG_DK).astype(BF16)
    sq = o * o
    hi = sq.astype(BF16)
    lo = (sq - hi.astype(F32)).astype(BF16)
    ms = (jnp.dot(hi, bones, preferred_element_type=F32)
          + jnp.dot(lo, bones, preferred_element_type=F32)) * (1.0 / HG_DK)
    gate = hgg_ref[...]
    o_hg = o * lax.rsqrt(ms + RMS_EPS) * ng_ref[...] * (gate * jax.nn.sigmoid(gate))
    o_rg = (hf_ref[...] + hb_ref[...]) * _gelu_tanh(ry_ref[...])
    cat = jnp.concatenate([o_hg.astype(BF16), o_rg.astype(BF16), da_ref[...]], axis=1)
    mix = jnp.dot(cat, wout_ref[...], preferred_element_type=F32)
    x = _layer_norm(alpha * x_ref[...] + mix, l1g_ref[...], l1b_ref[...])

    q = jnp.dot(x.astype(BF16), wq_ref[...], preferred_element_type=F32) * (CA_HEAD_DIM ** -0.5)
    outs = []
    for h in range(N_HEADS):
        sl = slice(h * CA_HEAD_DIM, (h + 1) * CA_HEAD_DIM)
        s = lax.dot_general(q[:, sl].astype(BF16), k_ref[:, sl], NT_DIMS, preferred_element_type=F32)
        p = jnp.exp(s - jnp.max(s, axis=1, keepdims=True))
        l = jnp.sum(p, axis=1, keepdims=True)
        o = jnp.dot(p.astype(BF16), v_ref[:, sl], preferred_element_type=F32) / l
        outs.append(o.astype(BF16))
    ca = jnp.dot(jnp.concatenate(outs, axis=1), wo_ref[...], preferred_element_type=F32)
    out_ref[...] = _layer_norm(alpha * x + ca, l2g_ref[...], l2b_ref[...])


def _mix_ca(x2, o_f, o_b, hg_in, norm_g, h_f, h_b, rg_in, o_da, w_out, ln1_g, ln1_b,
            wq, kmem, vmem, wo, ln2_g, ln2_b, batch, seq, n_mem, alpha, tm):
    m = x2.shape[0]
    nt = seq // tm
    row = lambda w, cidx=0: pl.BlockSpec((tm, w), lambda b, i: (b * nt + i, cidx))
    vec = lambda w: pl.BlockSpec((1, w), lambda b, i: (0, 0))
    mem_blk = pl.BlockSpec((n_mem, D_MODEL), lambda b, i: (b, 0))
    return pl.pallas_call(
        functools.partial(_mix_ca_kernel, alpha=alpha),
        grid=(batch, nt),
        in_specs=[row(D_MODEL), row(HG_WIDTH), row(HG_WIDTH), row(HG_WIDTH, 4), vec(HG_WIDTH),
                  row(RG_WIDTH), row(RG_WIDTH), row(RG_WIDTH, 1), row(DA_WIDTH),
                  _const_spec(w_out.shape), vec(D_MODEL), vec(D_MODEL),
                  _const_spec(wq.shape), mem_blk, mem_blk, _const_spec(wo.shape), vec(D_MODEL), vec(D_MODEL)],
        out_specs=row(D_MODEL),
        out_shape=jax.ShapeDtypeStruct((m, D_MODEL), F32),
        compiler_params=_cparams(("parallel", "parallel")),
        name="mix_cross_attn",
    )(x2, o_f, o_b, hg_in, norm_g, h_f, h_b, rg_in, o_da, w_out, ln1_g, ln1_b,
      wq, kmem, vmem, wo, ln2_g, ln2_b)


def _matmul_kernel(a_ref, w_ref, o_ref):
    o_ref[...] = jnp.dot(a_ref[...].astype(BF16), w_ref[...], preferred_element_type=F32).astype(o_ref.dtype)


def _mem_proj(mem2, w):
    m, n = mem2.shape[0], w.shape[1]
    return pl.pallas_call(
        _matmul_kernel,
        grid=(1,),
        in_specs=[pl.BlockSpec(mem2.shape, lambda i: (0, 0)), pl.BlockSpec(w.shape, lambda i: (0, 0))],
        out_specs=pl.BlockSpec((m, n), lambda i: (0, 0)),
        out_shape=jax.ShapeDtypeStruct((m, n), BF16),
        compiler_params=_cparams(("arbitrary",)),
        name="mem_proj",
    )(mem2, w)


def _ffn_kernel(xp_ref, x_ref, xn_ref, wu_ref, cw_ref, cb_ref, wd_ref, lg_ref, lb_ref, out_ref,
                *, tm, nt, alpha):
    pos = pl.program_id(1)
    x = x_ref[...]
    xp = jnp.where(pos > 0, xp_ref[...], 0.0)
    xn = jnp.where(pos < nt - 1, xn_ref[...], 0.0)
    xe = jnp.concatenate([xp, x, xn], axis=0).astype(BF16)
    gate = jnp.dot(xe, wu_ref[:, :D_FF], preferred_element_type=F32)
    cw = cw_ref[...]
    ne = tm + 2 * SUBLANES
    mid = slice(SUBLANES, SUBLANES + tm)
    gc = (cw[0:1] * pltpu.roll(gate, 1, 0)[mid] + cw[1:2] * gate[mid]
          + cw[2:3] * pltpu.roll(gate, ne - 1, 0)[mid] + cb_ref[...])
    val = jnp.dot(x.astype(BF16), wu_ref[:, D_FF:], preferred_element_type=F32)
    hid = (_gelu_tanh(gc) * val).astype(BF16)
    ff = jnp.dot(hid, wd_ref[...], preferred_element_type=F32)
    out_ref[...] = _layer_norm(alpha * x + ff, lg_ref[...], lb_ref[...])


def _ffn(x2, wu, conv_w, conv_b, wd, ln_g, ln_b, batch, seq, alpha, tm):
    m = x2.shape[0]
    nt = seq // tm
    per8 = tm // SUBLANES
    last8 = m // SUBLANES - 1
    blk = lambda b, i: b * nt + i
    vec = pl.BlockSpec((1, D_MODEL), lambda b, i: (0, 0))
    return pl.pallas_call(
        functools.partial(_ffn_kernel, tm=tm, nt=nt, alpha=alpha),
        grid=(batch, nt),
        in_specs=[pl.BlockSpec((SUBLANES, D_MODEL), lambda b, i: (jnp.maximum(blk(b, i) * per8 - 1, 0), 0)),
                  pl.BlockSpec((tm, D_MODEL), lambda b, i: (blk(b, i), 0)),
                  pl.BlockSpec((SUBLANES, D_MODEL), lambda b, i: (jnp.minimum((blk(b, i) + 1) * per8, last8), 0)),
                  _const_spec(wu.shape),
                  pl.BlockSpec(conv_w.shape, lambda b, i: (0, 0)),
                  pl.BlockSpec(conv_b.shape, lambda b, i: (0, 0)),
                  _const_spec(wd.shape), vec, vec],
        out_specs=pl.BlockSpec((tm, D_MODEL), lambda b, i: (blk(b, i), 0)),
        out_shape=jax.ShapeDtypeStruct((m, D_MODEL), F32),
        compiler_params=_cparams(("parallel", "parallel")),
        name="conv_glu",
    )(x2, x2, x2, wu, conv_w, conv_b, wd, ln_g, ln_b)


def _block_diag(w):
    n, blk, _ = w.shape
    eye = jnp.eye(n, dtype=w.dtype)
    return jnp.einsum("nij,nm->nimj", w, eye).reshape(n * blk, n * blk)


def _pick(seq, pref):
    t = min(seq, pref)
    assert seq % t == 0, (seq, pref)
    return t


def kernel(x, mem, positions, w_in, hg_lower_bounds, hg_norm_g, rg_conv_w, rg_conv_b, rg_wa, rg_ba, rg_wx, rg_bx, rg_lambda, da_lambda, da_subln_g, w_out, ln1_g, ln1_b, ca_wq, ca_wk, ca_wv, ca_wo, ln2_g, ln2_b, ffn_w_up, ffn_conv_w, ffn_conv_b, ffn_w_down, ln3_g, ln3_b):
    batch, seq, _ = x.shape
    n_mem = mem.shape[1]
    depth = w_in.shape[0]
    alpha = (2 * depth) ** 0.25
    m = batch * seq

    tm_proj = _pick(seq, 512)
    tm_ffn = _pick(seq, 256)
    tq = _pick(seq, 256)
    tb_hg = _pick(seq, 128)
    tb_rg = _pick(seq, 256)

    inv_freq = 1.0 / (ROPE_THETA ** (jnp.arange(0, DA_QK_DIM, 2, dtype=F32) / DA_QK_DIM))
    ang = positions.astype(F32).reshape(m, 1) * inv_freq
    cos, sin = jnp.cos(ang), jnp.sin(ang)
    cos_t = jnp.tile(cos, (1, 4))
    sin_t = jnp.tile(jnp.concatenate([-sin, sin], axis=1), (1, 2))

    lb_soft = jax.nn.softmax(hg_lower_bounds.astype(F32), axis=0)
    lb_all = jnp.cumsum(lb_soft, axis=0) - lb_soft[0:1]

    x2 = x.reshape(m, D_MODEL)
    mem2 = mem.reshape(batch * n_mem, D_MODEL)
    row = lambda v: v.reshape(1, -1).astype(F32)

    for layer in range(depth):
        hg_in, rg_in, qt, kk, vt = _inproj(x2, w_in[layer].astype(BF16), cos_t, sin_t, tm_proj)

        lam_init = 0.8 - 0.6 * math.exp(-0.3 * layer)
        lp = da_lambda[layer].astype(F32)
        lam = jnp.exp(jnp.sum(lp[0] * lp[1])) - jnp.exp(jnp.sum(lp[2] * lp[3])) + lam_init
        o_da = _diff_attention(jnp.full((1, LANES), lam, F32), qt, kk, vt, row(da_subln_g[layer]),
                               batch, seq, 1.0 - lam_init, tq, tm_proj)

        o_f, o_b = _hgrn(hg_in, row(lb_all[layer, 0]), row(lb_all[layer, 1]), batch, seq, tb_hg, 16)

        nsp = -RG_C * jax.nn.softplus(-rg_lambda[layer].astype(F32))
        h_dirs = []
        for d in range(2):
            wg = jnp.concatenate([_block_diag(rg_wa[layer, d]), _block_diag(rg_wx[layer, d])], axis=1).astype(BF16)
            bg = jnp.concatenate([rg_ba[layer, d], rg_bx[layer, d]]).reshape(1, -1).astype(F32)
            h_dirs.append(_rglru(rg_in, rg_conv_w[layer].astype(F32), row(rg_conv_b[layer]), wg, bg,
                                 row(nsp[d]), batch, seq, d == 1, tb_rg))

        kmem = _mem_proj(mem2, ca_wk[layer].astype(BF16))
        vmem = _mem_proj(mem2, ca_wv[layer].astype(BF16))
        x2 = _mix_ca(x2, o_f, o_b, hg_in, row(hg_norm_g[layer]), h_dirs[0], h_dirs[1], rg_in, o_da,
                     w_out[layer].astype(BF16), row(ln1_g[layer]), row(ln1_b[layer]),
                     ca_wq[layer].astype(BF16), kmem, vmem, ca_wo[layer].astype(BF16),
                     row(ln2_g[layer]), row(ln2_b[layer]), batch, seq, n_mem, alpha, tm_proj)

        x2 = _ffn(x2, ffn_w_up[layer].astype(BF16), ffn_conv_w[layer].astype(F32), row(ffn_conv_b[layer]),
                  ffn_w_down[layer].astype(BF16), row(ln3_g[layer]), row(ln3_b[layer]), batch, seq, alpha, tm_ffn)

    return x2.reshape(batch, seq, D_MODEL)
```

```python
import functools
import math

import jax
import jax.numpy as jnp
from jax import lax
from jax.experimental import pallas as pl
from jax.experimental.pallas import tpu as pltpu

F32 = jnp.float32
BF16 = jnp.bfloat16

D_MODEL = 1024
N_HEADS = 4
HG_WIDTH = 256
HG_DK = 64
RG_WIDTH = 256
RG_BLOCK = 64
RG_C = 8.0
DA_WIDTH = 512
DA_V_DIM = 128
DA_QK_DIM = 64
ROPE_THETA = 10000.0
CA_HEAD_DIM = 256
D_FF = 2816
LN_EPS = 1e-5
RMS_EPS = 1e-6

LANES = 128
SUBLANES = 8
VMEM_LIMIT = 56 * 1024 * 1024

NT_DIMS = (((1,), (1,)), ((), ()))
TN_DIMS = (((0,), (0,)), ((), ()))
NEG_BIG = -1e30
TINY = 1e-30
LOG2_E = math.log2(math.e)
ATTN_UNROLL = 16
BF16_ROWS = 16
VT_ROWS = DA_V_DIM + BF16_ROWS


def _cparams(sem, flags=None):
    return pltpu.CompilerParams(dimension_semantics=sem, vmem_limit_bytes=VMEM_LIMIT, flags=flags)


def _const_spec(shape):
    nd = len(shape)
    return pl.BlockSpec(shape, lambda *_: (0,) * nd, pipeline_mode=pl.Buffered(1))


def _layer_norm(y, g, b):
    mu = jnp.mean(y, axis=-1, keepdims=True)
    d = y - mu
    var = jnp.mean(d * d, axis=-1, keepdims=True)
    return d * lax.rsqrt(var + LN_EPS) * g + b


def _gelu_tanh(x):
    return 0.5 * x * (1.0 + jnp.tanh(math.sqrt(2.0 / math.pi) * (x + 0.044715 * (x * x * x))))


def _head_block_ones(n, blk):
    r = lax.broadcasted_iota(jnp.int32, (n, n), 0) // blk
    c = lax.broadcasted_iota(jnp.int32, (n, n), 1) // blk
    return r == c


def _inproj_kernel(x_ref, w_ref, cos_ref, sin_ref, hg_ref, rg_ref, qm_ref, k_ref, v_ref):
    xb = x_ref[...].astype(BF16)
    o_rg = 5 * HG_WIDTH
    o_q = o_rg + 2 * RG_WIDTH
    o_k, o_v = o_q + DA_WIDTH, o_q + 2 * DA_WIDTH
    proj = lambda lo, hi: jnp.dot(xb, w_ref[:, lo:hi], preferred_element_type=F32)
    hg_ref[...] = proj(0, o_rg)
    rg_ref[...] = proj(o_rg, o_q)

    reps = DA_WIDTH // LANES
    c = jnp.concatenate([cos_ref[...]] * reps, axis=1)
    sgn_sin = jnp.concatenate([sin_ref[...]] * reps, axis=1)
    half = DA_QK_DIM // 2
    lane = lax.broadcasted_iota(jnp.int32, c.shape, 1)
    first_half = (lane % DA_QK_DIM) < half

    def rot(t):
        partner = jnp.where(first_half, pltpu.roll(t, DA_WIDTH - half, 1), pltpu.roll(t, half, 1))
        return t * c + partner * sgn_sin

    q = rot(proj(o_q, o_k)) * (DA_QK_DIM ** -0.5 * LOG2_E)
    qt = q.T
    sub = lax.broadcasted_iota(jnp.int32, qt.shape, 0)
    first = (sub % DA_V_DIM) < DA_QK_DIM
    qm_ref[0] = jnp.where(first, qt, 0.0).astype(BF16)
    qm_ref[1] = jnp.where(first, 0.0, qt).astype(BF16)
    k_ref[...] = rot(proj(o_k, o_v)).astype(BF16)
    vt = proj(o_v, o_v + DA_WIDTH).T.astype(BF16)
    ones = jnp.ones((VT_ROWS - DA_V_DIM, vt.shape[1]), BF16)
    for h in range(N_HEADS):
        v_ref[0, h * VT_ROWS:h * VT_ROWS + DA_V_DIM, :] = vt[h * DA_V_DIM:(h + 1) * DA_V_DIM]
        v_ref[0, h * VT_ROWS + DA_V_DIM:(h + 1) * VT_ROWS, :] = ones


def _inproj(x2, w, cos_t, sin_t, tm):
    m = x2.shape[0]
    row = lambda w: pl.BlockSpec((tm, w), lambda i: (i, 0))
    return pl.pallas_call(
        _inproj_kernel,
        grid=(m // tm,),
        in_specs=[row(D_MODEL), _const_spec(w.shape), row(LANES), row(LANES)],
        out_specs=[row(5 * HG_WIDTH), row(2 * RG_WIDTH),
                   pl.BlockSpec((2, DA_WIDTH, tm), lambda i: (0, 0, i)), row(DA_WIDTH),
                   pl.BlockSpec((1, N_HEADS * VT_ROWS, tm), lambda i: (i, 0, 0))],
        out_shape=[jax.ShapeDtypeStruct((m, 5 * HG_WIDTH), F32),
                   jax.ShapeDtypeStruct((m, 2 * RG_WIDTH), F32),
                   jax.ShapeDtypeStruct((2, DA_WIDTH, m), BF16),
                   jax.ShapeDtypeStruct((m, DA_WIDTH), BF16),
                   jax.ShapeDtypeStruct((m // tm, N_HEADS * VT_ROWS, tm), BF16)],
        compiler_params=_cparams(("parallel",)),
        name="inproj",
    )(x2, w, cos_t, sin_t)


def _attn_kernel(lam_ref, qt_ref, qn_ref, k_ref, vt_ref, g_ref, o_ref, q_sc, s_sc, acc_sc,
                 *, tq, tk, nk, unroll, out_scale):
    q_sc[0] = jnp.concatenate([qt_ref[0], qt_ref[1]], axis=1)
    q_sc[1] = jnp.concatenate([qn_ref[0], qn_ref[1]], axis=1)
    acc_sc[...] = jnp.zeros(acc_sc.shape, F32)

    def scores(idx, slot, tile):
        kc = k_ref[pl.ds(pl.multiple_of(idx * tk, tk), tk), :]
        s_sc[slot] = jnp.dot(kc, q_sc[tile], preferred_element_type=F32)

    def absorb(idx, slot, m_prev):
        s = s_sc[slot]
        m_next = jnp.maximum(m_prev, jnp.max(s, axis=0, keepdims=True))
        p = jnp.exp2(s - m_next).astype(BF16)
        pv = jnp.dot(vt_ref[idx], p, preferred_element_type=F32)
        acc_sc[...] = jnp.exp2(m_prev - m_next) * acc_sc[...] + pv
        return m_next

    @pl.when(pl.program_id(2) == 0)
    def _():
        scores(0, 0, 0)

    def body(t, m_run):
        for u in range(unroll):
            c = unroll * t + u
            wrap = (c + 1 == nk).astype(jnp.int32)
            scores((c + 1) * (1 - wrap), (u + 1) % 2, wrap)
            m_run = absorb(c, u % 2, m_run)
        return m_run

    lax.fori_loop(0, nk // unroll, body, jnp.full((1, 2 * tq), -jnp.inf, F32))
    o = acc_sc[:DA_V_DIM, :] / acc_sc[DA_V_DIM:DA_V_DIM + 1, :]
    d = o[:, :tq] - lam_ref[...][:, :1] * o[:, tq:]
    ms = jnp.mean(d * d, axis=0, keepdims=True)
    gain = jnp.concatenate([g_ref[...]] * (tq // LANES), axis=1)
    o_ref[...] = (d * lax.rsqrt(ms + RMS_EPS) * gain * out_scale).astype(o_ref.dtype)


def _diff_attention(lam_row, qt, kk, vt, subln_g, batch, seq, out_scale, tq, tk):
    m = batch * seq
    nq = seq // tq
    nk = seq // tk
    unroll = math.gcd(nk, ATTN_UNROLL)
    assert unroll % 2 == 0, nk
    kern = functools.partial(_attn_kernel, tq=tq, tk=tk, nk=nk, unroll=unroll, out_scale=out_scale)
    return pl.pallas_call(
        kern,
        grid=(batch, N_HEADS, nq),
        in_specs=[pl.BlockSpec((1, LANES), lambda b, h, i: (0, 0)),
                  pl.BlockSpec((2, DA_V_DIM, tq), lambda b, h, i: (0, h, b * nq + i)),
                  pl.BlockSpec((2, DA_V_DIM, tq), lambda b, h, i: (0, h, b * nq + jnp.minimum(i + 1, nq - 1))),
                  pl.BlockSpec((seq, DA_V_DIM), lambda b, h, i: (b, h)),
                  pl.BlockSpec((nk, VT_ROWS, tk), lambda b, h, i: (b, h, 0)),
                  pl.BlockSpec((DA_V_DIM, LANES), lambda b, h, i: (0, 0))],
        out_specs=pl.BlockSpec((DA_V_DIM, tq), lambda b, h, i: (h, b * nq + i)),
        out_shape=jax.ShapeDtypeStruct((DA_WIDTH, m), BF16),
        scratch_shapes=[pltpu.VMEM((2, DA_V_DIM, 2 * tq), BF16), pltpu.VMEM((2, tk, 2 * tq), F32),
                        pltpu.VMEM((VT_ROWS, 2 * tq), F32)],
        compiler_params=_cparams(("parallel", "parallel", "arbitrary")),
        name="diff_attn",
    )(lam_row, qt, qt, kk, vt, subln_g)


def _hgrn_kernel(qf_ref, ff_ref, vf_ref, lbf_ref, qb_ref, fb_ref, vb_ref, lbb_ref, of_ref, ob_ref,
                 stf_sc, stb_sc, *, tb, chunk):
    @pl.when(pl.program_id(1) == 0)
    def _():
        stf_sc[...] = jnp.zeros(stf_sc.shape, F32)
        stb_sc[...] = jnp.zeros(stb_sc.shape, F32)

    _hgrn_direction(qf_ref, ff_ref, vf_ref, lbf_ref, of_ref, stf_sc, tb, chunk, False)
    _hgrn_direction(qb_ref, fb_ref, vb_ref, lbb_ref, ob_ref, stb_sc, tb, chunk, True)


def _hgrn_direction(q_ref, f_ref, v_ref, lb_ref, o_ref, st_sc, tb, chunk, reverse):
    lb = lb_ref[...]
    z = f_ref[...]
    qraw = q_ref[...]
    qs = qraw * jax.nn.sigmoid(qraw)
    g = jnp.log(lb + (1.0 - lb) * jax.nn.sigmoid(z))
    kfull = (1.0 - lb) * jax.nn.sigmoid(-z)
    vfull = v_ref[...]

    r = lax.broadcasted_iota(jnp.int32, (tb, tb), 0)
    c = lax.broadcasted_iota(jnp.int32, (tb, tb), 1)
    tri = ((r // chunk) == (c // chunk)) & ((c >= r) if reverse else (c <= r))
    b = jnp.dot(tri.astype(F32), g, precision=lax.Precision.HIGHEST, preferred_element_type=F32)

    bones = _head_block_ones(HG_WIDTH, HG_DK).astype(BF16)
    pair_head = _head_block_ones(LANES, HG_DK)
    half = chunk // 2
    rows = lax.broadcasted_iota(jnp.int32, (half, HG_WIDTH), 0)
    zeros_half = jnp.zeros((half, HG_WIDTH), F32)

    order = range(tb // chunk - 1, -1, -1) if reverse else range(tb // chunk)
    for ci in order:
        sl = slice(ci * chunk, (ci + 1) * chunk)
        bc, qc, kc, vc = b[sl], qs[sl], kfull[sl], vfull[sl]
        b_end = bc[0:1] if reverse else bc[chunk - 1:chunk]
        qt = (qc * jnp.exp(bc)).astype(BF16)
        kh = (kc * jnp.exp(b_end - bc)).astype(BF16)
        vb = vc.astype(BF16)
        dec = jnp.exp(b_end)
        inter = []
        for p in range(HG_WIDTH // LANES):
            ln = slice(p * LANES, (p + 1) * LANES)
            st = st_sc[p]
            inter.append(lax.dot_general(qt[:, ln], st.astype(BF16), NT_DIMS, preferred_element_type=F32))
            kv = lax.dot_general(vb[:, ln], kh[:, ln], TN_DIMS, preferred_element_type=F32)
            st_sc[p] = jnp.where(pair_head, st * dec[:, ln] + kv, 0.0)
        o_c = jnp.concatenate(inter, axis=1)
        pieces = []
        for s in range(chunk):
            parts = []
            for hf in range(2):
                lo = hf * half
                live = (lo + half - 1 >= s) if not reverse else (lo <= s)
                if not live:
                    parts.append(zeros_half)
                    continue
                diff = bc[lo:lo + half] - bc[s:s + 1]
                all_kept = (lo + half - 1 <= s) if reverse else (lo >= s)
                if not all_kept:
                    keep = (rows + lo <= s) if reverse else (rows + lo >= s)
                    diff = jnp.where(keep, diff, NEG_BIG)
                e = jnp.exp(diff)
                parts.append(qc[lo:lo + half] * e * kc[s:s + 1])
            pieces.append(jnp.concatenate(parts, axis=0).astype(BF16))
        att = jnp.dot(jnp.concatenate(pieces, axis=0), bones, preferred_element_type=F32)
        for s in range(chunk):
            o_c = o_c + att[s * chunk:(s + 1) * chunk] * vc[s:s + 1]
        o_ref[sl, :] = o_c


def _hgrn(hg_in, lb_fwd, lb_bwd, batch, seq, tb, chunk):
    m = batch * seq
    nb = seq // tb
    fwd = lambda cidx: pl.BlockSpec((tb, HG_WIDTH), lambda b, j: (b * nb + j, cidx))
    bwd = lambda cidx: pl.BlockSpec((tb, HG_WIDTH), lambda b, j: (b * nb + nb - 1 - j, cidx))
    vec = pl.BlockSpec((1, HG_WIDTH), lambda b, j: (0, 0))
    state = pltpu.VMEM((HG_WIDTH // LANES, LANES, LANES), F32)
    return pl.pallas_call(
        functools.partial(_hgrn_kernel, tb=tb, chunk=chunk),
        grid=(batch, nb),
        in_specs=[fwd(0), fwd(1), fwd(3), vec, bwd(0), bwd(2), bwd(3), vec],
        out_specs=[fwd(0), bwd(0)],
        out_shape=[jax.ShapeDtypeStruct((m, HG_WIDTH), F32)] * 2,
        scratch_shapes=[state, state],
        compiler_params=_cparams(("parallel", "arbitrary")),
        name="hgrn",
    )(hg_in, hg_in, hg_in, lb_fwd, hg_in, hg_in, hg_in, lb_bwd)


def _rglru_kernel(xp_ref, x_ref, xn_ref, cw_ref, cb_ref, wg_ref, bg_ref, nsp_ref, h_ref, carry_sc,
                  *, tb, nb, reverse):
    j = pl.program_id(1)
    pos = (nb - 1 - j) if reverse else j

    @pl.when(j == 0)
    def _():
        carry_sc[...] = jnp.zeros(carry_sc.shape, F32)

    xp = jnp.where(pos > 0, xp_ref[...], 0.0)
    xn = jnp.where(pos < nb - 1, xn_ref[...], 0.0)
    xe = jnp.concatenate([xp, x_ref[...], xn], axis=0)
    cw = cw_ref[...]
    ne = tb + 2 * SUBLANES
    mid = slice(SUBLANES, SUBLANES + tb)
    tap = lambda k: pltpu.roll(xe, (-k) % ne, 0)[mid]
    xc = cw[0:1] * tap(-2) + cw[1:2] * tap(-1) + cw[2:3] * xe[mid] + cw[3:4] * tap(1) + cb_ref[...]
    gates = jnp.dot(xc.astype(BF16), wg_ref[...], preferred_element_type=F32) + bg_ref[...]
    rgate = jax.nn.sigmoid(gates[:, :RG_WIDTH])
    igate = jax.nn.sigmoid(gates[:, RG_WIDTH:])
    log_a = nsp_ref[...] * rgate
    a = jnp.exp(log_a)
    one_m_a2 = 1.0 - a * a
    u = one_m_a2 * lax.rsqrt(jnp.maximum(one_m_a2, TINY)) * (igate * xc)

    rows = lax.broadcasted_iota(jnp.int32, (tb, RG_WIDTH), 0) % SUBLANES
    d = 1
    while d < SUBLANES:
        if reverse:
            a_sh, u_sh, ok = pltpu.roll(a, tb - d, 0), pltpu.roll(u, tb - d, 0), rows < SUBLANES - d
        else:
            a_sh, u_sh, ok = pltpu.roll(a, d, 0), pltpu.roll(u, d, 0), rows >= d
        u = jnp.where(ok, a * u_sh + u, u)
        a = jnp.where(ok, a * a_sh, a)
        d *= 2
    ngroups = tb // SUBLANES
    carry = carry_sc[...]
    out = [None] * ngroups
    for gi in (range(ngroups - 1, -1, -1) if reverse else range(ngroups)):
        sl = slice(gi * SUBLANES, (gi + 1) * SUBLANES)
        hg = a[sl] * carry + u[sl]
        out[gi] = hg
        carry = hg[0:1] if reverse else hg[SUBLANES - 1:SUBLANES]
    h_ref[...] = jnp.concatenate(out, axis=0)
    carry_sc[...] = carry


def _rglru(rg_in, conv_w, conv_b, wg, bg, nsp, batch, seq, reverse, tb):
    m = batch * seq
    nb = seq // tb
    per8 = tb // SUBLANES
    last8 = m // SUBLANES - 1
    pos = (lambda j: nb - 1 - j) if reverse else (lambda j: j)
    blk = lambda b, j: b * nb + pos(j)
    kern = functools.partial(_rglru_kernel, tb=tb, nb=nb, reverse=reverse)
    return pl.pallas_call(
        kern,
        grid=(batch, nb),
        in_specs=[pl.BlockSpec((SUBLANES, RG_WIDTH), lambda b, j: (jnp.maximum(blk(b, j) * per8 - 1, 0), 0)),
                  pl.BlockSpec((tb, RG_WIDTH), lambda b, j: (blk(b, j), 0)),
                  pl.BlockSpec((SUBLANES, RG_WIDTH), lambda b, j: (jnp.minimum((blk(b, j) + 1) * per8, last8), 0)),
                  pl.BlockSpec(conv_w.shape, lambda b, j: (0, 0)),
                  pl.BlockSpec(conv_b.shape, lambda b, j: (0, 0)),
                  pl.BlockSpec(wg.shape, lambda b, j: (0, 0)),
                  pl.BlockSpec(bg.shape, lambda b, j: (0, 0)),
                  pl.BlockSpec(nsp.shape, lambda b, j: (0, 0))],
        out_specs=pl.BlockSpec((tb, RG_WIDTH), lambda b, j: (blk(b, j), 0)),
        out_shape=jax.ShapeDtypeStruct((m, RG_WIDTH), F32),
        scratch_shapes=[pltpu.VMEM((1, RG_WIDTH), F32)],
        compiler_params=_cparams(("parallel", "arbitrary")),
        name="rglru_bwd" if reverse else "rglru_fwd",
    )(rg_in, rg_in, rg_in, conv_w, conv_b, wg, bg, nsp)


def _mix_ca_kernel(x_ref, of_ref, ob_ref, hgg_ref, ng_ref, hf_ref, hb_ref, ry_ref, da_ref, wout_ref,
                   l1g_ref, l1b_ref, wq_ref, k_ref, v_ref, wo_ref, l2g_ref, l2b_ref, out_ref, *, alpha):
    o = of_ref[...] + ob_ref[...]
    bones = _head_block_ones(HG_WIDTH, HG_DK).astype(BF16)
    sq = o * o
    hi = sq.astype(BF16)
    lo = (sq - hi.astype(F32)).astype(BF16)
    ms = (jnp.dot(hi, bones, preferred_element_type=F32)
          + jnp.dot(lo, bones, preferred_element_type=F32)) * (1.0 / HG_DK)
    gate = hgg_ref[...]
    o_hg = o * lax.rsqrt(ms + RMS_EPS) * ng_ref[...] * (gate * jax.nn.sigmoid(gate))
    o_rg = (hf_ref[...] + hb_ref[...]) * _gelu_tanh(ry_ref[...])
    cat = jnp.concatenate([o_hg.astype(BF16), o_rg.astype(BF16)], axis=1)
    n_rec = HG_WIDTH + RG_WIDTH
    mix = (jnp.dot(cat, wout_ref[:n_rec, :], preferred_element_type=F32)
           + lax.dot_general(da_ref[...], wout_ref[n_rec:, :], TN_DIMS, preferred_element_type=F32))
    x = _layer_norm(alpha * x_ref[...] + mix, l1g_ref[...], l1b_ref[...])

    q = jnp.dot(x.astype(BF16), wq_ref[...], preferred_element_type=F32) * (CA_HEAD_DIM ** -0.5)
    outs = []
    for h in range(N_HEADS):
        sl = slice(h * CA_HEAD_DIM, (h + 1) * CA_HEAD_DIM)
        s = lax.dot_general(q[:, sl].astype(BF16), k_ref[:, sl], NT_DIMS, preferred_element_type=F32)
        p = jnp.exp(s - jnp.max(s, axis=1, keepdims=True))
        l = jnp.sum(p, axis=1, keepdims=True)
        o = jnp.dot(p.astype(BF16), v_ref[:, sl], preferred_element_type=F32) / l
        outs.append(o.astype(BF16))
    ca = jnp.dot(jnp.concatenate(outs, axis=1), wo_ref[...], preferred_element_type=F32)
    out_ref[...] = _layer_norm(alpha * x + ca, l2g_ref[...], l2b_ref[...])


def _mix_ca(x2, o_f, o_b, hg_in, norm_g, h_f, h_b, rg_in, o_da, w_out, ln1_g, ln1_b,
            wq, kmem, vmem, wo, ln2_g, ln2_b, batch, seq, n_mem, alpha, tm):
    m = x2.shape[0]
    nt = seq // tm
    row = lambda w, cidx=0: pl.BlockSpec((tm, w), lambda b, i: (b * nt + i, cidx))
    vec = lambda w: pl.BlockSpec((1, w), lambda b, i: (0, 0))
    mem_blk = pl.BlockSpec((n_mem, D_MODEL), lambda b, i: (b, 0))
    return pl.pallas_call(
        functools.partial(_mix_ca_kernel, alpha=alpha),
        grid=(batch, nt),
        in_specs=[row(D_MODEL), row(HG_WIDTH), row(HG_WIDTH), row(HG_WIDTH, 4), vec(HG_WIDTH),
                  row(RG_WIDTH), row(RG_WIDTH), row(RG_WIDTH, 1),
                  pl.BlockSpec((DA_WIDTH, tm), lambda b, i: (0, b * nt + i)),
                  _const_spec(w_out.shape), vec(D_MODEL), vec(D_MODEL),
                  _const_spec(wq.shape), mem_blk, mem_blk, _const_spec(wo.shape), vec(D_MODEL), vec(D_MODEL)],
        out_specs=row(D_MODEL),
        out_shape=jax.ShapeDtypeStruct((m, D_MODEL), F32),
        compiler_params=_cparams(("parallel", "parallel")),
        name="mix_cross_attn",
    )(x2, o_f, o_b, hg_in, norm_g, h_f, h_b, rg_in, o_da, w_out, ln1_g, ln1_b,
      wq, kmem, vmem, wo, ln2_g, ln2_b)


def _matmul_kernel(a_ref, w_ref, o_ref):
    o_ref[...] = jnp.dot(a_ref[...].astype(BF16), w_ref[...], preferred_element_type=F32).astype(o_ref.dtype)


def _mem_proj(mem2, w):
    m, n = mem2.shape[0], w.shape[1]
    return pl.pallas_call(
        _matmul_kernel,
        grid=(1,),
        in_specs=[pl.BlockSpec(mem2.shape, lambda i: (0, 0)), pl.BlockSpec(w.shape, lambda i: (0, 0))],
        out_specs=pl.BlockSpec((m, n), lambda i: (0, 0)),
        out_shape=jax.ShapeDtypeStruct((m, n), BF16),
        compiler_params=_cparams(("arbitrary",)),
        name="mem_proj",
    )(mem2, w)


def _ffn_kernel(xp_ref, x_ref, xn_ref, wu_ref, cw_ref, cb_ref, wd_ref, lg_ref, lb_ref, out_ref,
                *, tm, nt, alpha):
    pos = pl.program_id(1)
    x = x_ref[...]
    xp = jnp.where(pos > 0, xp_ref[...], 0.0)
    xn = jnp.where(pos < nt - 1, xn_ref[...], 0.0)
    xe = jnp.concatenate([xp, x, xn], axis=0).astype(BF16)
    gate = jnp.dot(xe, wu_ref[:, :D_FF], preferred_element_type=F32)
    cw = cw_ref[...]
    ne = tm + 2 * SUBLANES
    mid = slice(SUBLANES, SUBLANES + tm)
    gc = (cw[0:1] * pltpu.roll(gate, 1, 0)[mid] + cw[1:2] * gate[mid]
          + cw[2:3] * pltpu.roll(gate, ne - 1, 0)[mid] + cb_ref[...])
    val = jnp.dot(x.astype(BF16), wu_ref[:, D_FF:], preferred_element_type=F32)
    hid = (_gelu_tanh(gc) * val).astype(BF16)
    ff = jnp.dot(hid, wd_ref[...], preferred_element_type=F32)
    out_ref[...] = _layer_norm(alpha * x + ff, lg_ref[...], lb_ref[...])


def _ffn(x2, wu, conv_w, conv_b, wd, ln_g, ln_b, batch, seq, alpha, tm):
    m = x2.shape[0]
    nt = seq // tm
    per8 = tm // SUBLANES
    last8 = m // SUBLANES - 1
    blk = lambda b, i: b * nt + i
    vec = pl.BlockSpec((1, D_MODEL), lambda b, i: (0, 0))
    return pl.pallas_call(
        functools.partial(_ffn_kernel, tm=tm, nt=nt, alpha=alpha),
        grid=(batch, nt),
        in_specs=[pl.BlockSpec((SUBLANES, D_MODEL), lambda b, i: (jnp.maximum(blk(b, i) * per8 - 1, 0), 0)),
                  pl.BlockSpec((tm, D_MODEL), lambda b, i: (blk(b, i), 0)),
                  pl.BlockSpec((SUBLANES, D_MODEL), lambda b, i: (jnp.minimum((blk(b, i) + 1) * per8, last8), 0)),
                  _const_spec(wu.shape),
                  pl.BlockSpec(conv_w.shape, lambda b, i: (0, 0)),
                  pl.BlockSpec(conv_b.shape, lambda b, i: (0, 0)),
                  _const_spec(wd.shape), vec, vec],
        out_specs=pl.BlockSpec((tm, D_MODEL), lambda b, i: (blk(b, i), 0)),
        out_shape=jax.ShapeDtypeStruct((m, D_MODEL), F32),
        compiler_params=_cparams(("parallel", "parallel")),
        name="conv_glu",
    )(x2, x2, x2, wu, conv_w, conv_b, wd, ln_g, ln_b)


def _block_diag(w):
    n, blk, _ = w.shape
    eye = jnp.eye(n, dtype=w.dtype)
    return jnp.einsum("nij,nm->nimj", w, eye).reshape(n * blk, n * blk)


def _pick(seq, pref):
    t = min(seq, pref)
    assert seq % t == 0, (seq, pref)
    return t


def kernel(x, mem, positions, w_in, hg_lower_bounds, hg_norm_g, rg_conv_w, rg_conv_b, rg_wa, rg_ba, rg_wx, rg_bx, rg_lambda, da_lambda, da_subln_g, w_out, ln1_g, ln1_b, ca_wq, ca_wk, ca_wv, ca_wo, ln2_g, ln2_b, ffn_w_up, ffn_conv_w, ffn_conv_b, ffn_w_down, ln3_g, ln3_b):
    batch, seq, _ = x.shape
    n_mem = mem.shape[1]
    depth = w_in.shape[0]
    alpha = (2 * depth) ** 0.25
    m = batch * seq

    tm_proj = _pick(seq, 512)
    tm_ffn = _pick(seq, 256)
    tq = _pick(seq, 256)
    tb_hg = _pick(seq, 128)
    tb_rg = _pick(seq, 256)

    inv_freq = 1.0 / (ROPE_THETA ** (jnp.arange(0, DA_QK_DIM, 2, dtype=F32) / DA_QK_DIM))
    ang = positions.astype(F32).reshape(m, 1) * inv_freq
    cos, sin = jnp.cos(ang), jnp.sin(ang)
    cos_t = jnp.tile(cos, (1, 4))
    sin_t = jnp.tile(jnp.concatenate([-sin, sin], axis=1), (1, 2))

    lb_soft = jax.nn.softmax(hg_lower_bounds.astype(F32), axis=0)
    lb_all = jnp.cumsum(lb_soft, axis=0) - lb_soft[0:1]

    x2 = x.reshape(m, D_MODEL)
    mem2 = mem.reshape(batch * n_mem, D_MODEL)
    row = lambda v: v.reshape(1, -1).astype(F32)

    for layer in range(depth):
        hg_in, rg_in, qt, kk, vt = _inproj(x2, w_in[layer].astype(BF16), cos_t, sin_t, tm_proj)

        lam_init = 0.8 - 0.6 * math.exp(-0.3 * layer)
        lp = da_lambda[layer].astype(F32)
        lam = jnp.exp(jnp.sum(lp[0] * lp[1])) - jnp.exp(jnp.sum(lp[2] * lp[3])) + lam_init
        gain = jnp.broadcast_to(da_subln_g[layer].astype(F32)[:, None], (DA_V_DIM, LANES))
        o_da = _diff_attention(jnp.full((1, LANES), lam, F32), qt, kk, vt, gain,
                               batch, seq, 1.0 - lam_init, tq, tm_proj)

        o_f, o_b = _hgrn(hg_in, row(lb_all[layer, 0]), row(lb_all[layer, 1]), batch, seq, tb_hg, 16)

        nsp = -RG_C * jax.nn.softplus(-rg_lambda[layer].astype(F32))
        h_dirs = []
        for d in range(2):
            wg = jnp.concatenate([_block_diag(rg_wa[layer, d]), _block_diag(rg_wx[layer, d])], axis=1).astype(BF16)
            bg = jnp.concatenate([rg_ba[layer, d], rg_bx[layer, d]]).reshape(1, -1).astype(F32)
            h_dirs.append(_rglru(rg_in, rg_conv_w[layer].astype(F32), row(rg_conv_b[layer]), wg, bg,
                                 row(nsp[d]), batch, seq, d == 1, tb_rg))

        kmem = _mem_proj(mem2, ca_wk[layer].astype(BF16))
        vmem = _mem_proj(mem2, ca_wv[layer].astype(BF16))
        x2 = _mix_ca(x2, o_f, o_b, hg_in, row(hg_norm_g[layer]), h_dirs[0], h_dirs[1], rg_in, o_da,
                     w_out[layer].astype(BF16), row(ln1_g[layer]), row(ln1_b[layer]),
                     ca_wq[layer].astype(BF16), kmem, vmem, ca_wo[layer].astype(BF16),
                     row(ln2_g[layer]), row(ln2_b[layer]), batch, seq, n_mem, alpha, tm_proj)

        x2 = _ffn(x2, ffn_w_up[layer].astype(BF16), ffn_conv_w[layer].astype(F32), row(ffn_conv_b[layer]),
                  ffn_w_down[layer].astype(BF16), row(ln3_g[layer]), row(ln3_b[layer]), batch, seq, alpha, tm_ffn)

    return x2.reshape(batch, seq, D_MODEL)
```

```python
import functools
import math

import jax
import jax.numpy as jnp
from jax import lax
from jax.experimental import pallas as pl
from jax.experimental.pallas import tpu as pltpu

F32 = jnp.float32
BF16 = jnp.bfloat16

D_MODEL = 1024
N_HEADS = 4
HG_WIDTH = 256
HG_DK = 64
RG_WIDTH = 256
RG_BLOCK = 64
RG_C = 8.0
DA_WIDTH = 512
DA_V_DIM = 128
DA_QK_DIM = 64
ROPE_THETA = 10000.0
CA_HEAD_DIM = 256
D_FF = 2816
LN_EPS = 1e-5
RMS_EPS = 1e-6

LANES = 128
SUBLANES = 8
VMEM_LIMIT = 56 * 1024 * 1024

NT_DIMS = (((1,), (1,)), ((), ()))
TN_DIMS = (((0,), (0,)), ((), ()))
NEG_BIG = -1e30
TINY = 1e-30
LOG2_E = math.log2(math.e)
ATTN_UNROLL = 16
BF16_ROWS = 16
VT_ROWS = DA_V_DIM + BF16_ROWS


def _cparams(sem, flags=None):
    return pltpu.CompilerParams(dimension_semantics=sem, vmem_limit_bytes=VMEM_LIMIT, flags=flags)


def _const_spec(shape):
    nd = len(shape)
    return pl.BlockSpec(shape, lambda *_: (0,) * nd, pipeline_mode=pl.Buffered(1))


def _layer_norm(y, g, b):
    mu = jnp.mean(y, axis=-1, keepdims=True)
    d = y - mu
    var = jnp.mean(d * d, axis=-1, keepdims=True)
    return d * lax.rsqrt(var + LN_EPS) * g + b


def _gelu_tanh(x):
    return 0.5 * x * (1.0 + jnp.tanh(math.sqrt(2.0 / math.pi) * (x + 0.044715 * (x * x * x))))


def _head_block_ones(n, blk):
    r = lax.broadcasted_iota(jnp.int32, (n, n), 0) // blk
    c = lax.broadcasted_iota(jnp.int32, (n, n), 1) // blk
    return r == c


def _inproj_kernel(x_ref, w_ref, cos_ref, sin_ref, hg_ref, rg_ref, qm_ref, k_ref, v_ref):
    xb = x_ref[...].astype(BF16)
    o_rg = 5 * HG_WIDTH
    o_q = o_rg + 2 * RG_WIDTH
    o_k, o_v = o_q + DA_WIDTH, o_q + 2 * DA_WIDTH
    proj = lambda lo, hi: jnp.dot(xb, w_ref[:, lo:hi], preferred_element_type=F32)
    hg_ref[...] = proj(0, o_rg)
    rg_ref[...] = proj(o_rg, o_q)

    reps = DA_WIDTH // LANES
    c = jnp.concatenate([cos_ref[...]] * reps, axis=1)
    sgn_sin = jnp.concatenate([sin_ref[...]] * reps, axis=1)
    half = DA_QK_DIM // 2
    lane = lax.broadcasted_iota(jnp.int32, c.shape, 1)
    first_half = (lane % DA_QK_DIM) < half

    def rot(t):
        partner = jnp.where(first_half, pltpu.roll(t, DA_WIDTH - half, 1), pltpu.roll(t, half, 1))
        return t * c + partner * sgn_sin

    q = rot(proj(o_q, o_k)) * (DA_QK_DIM ** -0.5 * LOG2_E)
    qt = q.T
    sub = lax.broadcasted_iota(jnp.int32, qt.shape, 0)
    first = (sub % DA_V_DIM) < DA_QK_DIM
    qm_ref[0] = jnp.where(first, qt, 0.0).astype(BF16)
    qm_ref[1] = jnp.where(first, 0.0, qt).astype(BF16)
    k_ref[...] = rot(proj(o_k, o_v)).astype(BF16)
    vt = proj(o_v, o_v + DA_WIDTH).T.astype(BF16)
    ones = jnp.ones((VT_ROWS - DA_V_DIM, vt.shape[1]), BF16)
    for h in range(N_HEADS):
        v_ref[0, h * VT_ROWS:h * VT_ROWS + DA_V_DIM, :] = vt[h * DA_V_DIM:(h + 1) * DA_V_DIM]
        v_ref[0, h * VT_ROWS + DA_V_DIM:(h + 1) * VT_ROWS, :] = ones


def _inproj(x2, w, cos_t, sin_t, tm):
    m = x2.shape[0]
    row = lambda w: pl.BlockSpec((tm, w), lambda i: (i, 0))
    return pl.pallas_call(
        _inproj_kernel,
        grid=(m // tm,),
        in_specs=[row(D_MODEL), _const_spec(w.shape), row(LANES), row(LANES)],
        out_specs=[row(5 * HG_WIDTH), row(2 * RG_WIDTH),
                   pl.BlockSpec((2, DA_WIDTH, tm), lambda i: (0, 0, i)), row(DA_WIDTH),
                   pl.BlockSpec((1, N_HEADS * VT_ROWS, tm), lambda i: (i, 0, 0))],
        out_shape=[jax.ShapeDtypeStruct((m, 5 * HG_WIDTH), F32),
                   jax.ShapeDtypeStruct((m, 2 * RG_WIDTH), F32),
                   jax.ShapeDtypeStruct((2, DA_WIDTH, m), BF16),
                   jax.ShapeDtypeStruct((m, DA_WIDTH), BF16),
                   jax.ShapeDtypeStruct((m // tm, N_HEADS * VT_ROWS, tm), BF16)],
        compiler_params=_cparams(("parallel",)),
        name="inproj",
    )(x2, w, cos_t, sin_t)


def _attn_kernel(lam_ref, qt_ref, qn_ref, k_ref, vt_ref, g_ref, o_ref, q_sc, s_sc, acc_sc,
                 *, tq, tk, nk, unroll, out_scale):
    q_sc[0] = jnp.concatenate([qt_ref[0], qt_ref[1]], axis=1)
    q_sc[1] = jnp.concatenate([qn_ref[0], qn_ref[1]], axis=1)
    acc_sc[...] = jnp.zeros(acc_sc.shape, F32)

    def scores(idx, slot, tile):
        kc = k_ref[pl.ds(pl.multiple_of(idx * tk, tk), tk), :]
        s_sc[slot] = jnp.dot(kc, q_sc[tile], preferred_element_type=F32)

    def absorb(idx, slot, m_prev):
        s = s_sc[slot]
        m_next = jnp.maximum(m_prev, jnp.max(s, axis=0, keepdims=True))
        p = jnp.exp2(s - m_next).astype(BF16)
        pv = jnp.dot(vt_ref[idx], p, preferred_element_type=F32)
        acc_sc[...] = jnp.exp2(m_prev - m_next) * acc_sc[...] + pv
        return m_next

    @pl.when(pl.program_id(2) == 0)
    def _():
        scores(0, 0, 0)

    def body(t, m_run):
        for u in range(unroll):
            c = unroll * t + u
            wrap = jnp.asarray(c + 1 == nk).astype(jnp.int32)
            scores((c + 1) * (1 - wrap), (u + 1) % 2, wrap)
            m_run = absorb(c, u % 2, m_run)
        return m_run

    lax.fori_loop(0, nk // unroll, body, jnp.full((1, 2 * tq), -jnp.inf, F32))
    o = acc_sc[:DA_V_DIM, :] / acc_sc[DA_V_DIM:DA_V_DIM + 1, :]
    d = o[:, :tq] - lam_ref[...][:, :1] * o[:, tq:]
    ms = jnp.mean(d * d, axis=0, keepdims=True)
    gain = jnp.concatenate([g_ref[...]] * (tq // LANES), axis=1)
    o_ref[...] = (d * lax.rsqrt(ms + RMS_EPS) * gain * out_scale).astype(o_ref.dtype)


def _diff_attention(lam_row, qt, kk, vt, subln_g, batch, seq, out_scale, tq, tk):
    m = batch * seq
    nq = seq // tq
    nk = seq // tk
    unroll = math.gcd(nk, ATTN_UNROLL)
    assert unroll % 2 == 0, nk
    kern = functools.partial(_attn_kernel, tq=tq, tk=tk, nk=nk, unroll=unroll, out_scale=out_scale)
    return pl.pallas_call(
        kern,
        grid=(batch, N_HEADS, nq),
        in_specs=[pl.BlockSpec((1, LANES), lambda b, h, i: (0, 0)),
                  pl.BlockSpec((2, DA_V_DIM, tq), lambda b, h, i: (0, h, b * nq + i)),
                  pl.BlockSpec((2, DA_V_DIM, tq), lambda b, h, i: (0, h, b * nq + jnp.minimum(i + 1, nq - 1))),
                  pl.BlockSpec((seq, DA_V_DIM), lambda b, h, i: (b, h)),
                  pl.BlockSpec((nk, VT_ROWS, tk), lambda b, h, i: (b, h, 0)),
                  pl.BlockSpec((DA_V_DIM, LANES), lambda b, h, i: (0, 0))],
        out_specs=pl.BlockSpec((DA_V_DIM, tq), lambda b, h, i: (h, b * nq + i)),
        out_shape=jax.ShapeDtypeStruct((DA_WIDTH, m), BF16),
        scratch_shapes=[pltpu.VMEM((2, DA_V_DIM, 2 * tq), BF16), pltpu.VMEM((2, tk, 2 * tq), F32),
                        pltpu.VMEM((VT_ROWS, 2 * tq), F32)],
        compiler_params=_cparams(("parallel", "parallel", "arbitrary")),
        name="diff_attn",
    )(lam_row, qt, qt, kk, vt, subln_g)


def _hgrn_kernel(qf_ref, ff_ref, vf_ref, lbf_ref, qb_ref, fb_ref, vb_ref, lbb_ref, of_ref, ob_ref,
                 stf_sc, stb_sc, *, tb, chunk):
    @pl.when(pl.program_id(1) == 0)
    def _():
        stf_sc[...] = jnp.zeros(stf_sc.shape, F32)
        stb_sc[...] = jnp.zeros(stb_sc.shape, F32)

    _hgrn_direction(qf_ref, ff_ref, vf_ref, lbf_ref, of_ref, stf_sc, tb, chunk, False)
    _hgrn_direction(qb_ref, fb_ref, vb_ref, lbb_ref, ob_ref, stb_sc, tb, chunk, True)


def _hgrn_direction(q_ref, f_ref, v_ref, lb_ref, o_ref, st_sc, tb, chunk, reverse):
    lb = lb_ref[...]
    z = f_ref[...]
    qraw = q_ref[...]
    qs = qraw * jax.nn.sigmoid(qraw)
    g = jnp.log(lb + (1.0 - lb) * jax.nn.sigmoid(z))
    kfull = (1.0 - lb) * jax.nn.sigmoid(-z)
    vfull = v_ref[...]

    r = lax.broadcasted_iota(jnp.int32, (tb, tb), 0)
    c = lax.broadcasted_iota(jnp.int32, (tb, tb), 1)
    tri = ((r // chunk) == (c // chunk)) & ((c >= r) if reverse else (c <= r))
    b = jnp.dot(tri.astype(F32), g, precision=lax.Precision.HIGHEST, preferred_element_type=F32)

    bones = _head_block_ones(HG_WIDTH, HG_DK).astype(BF16)
    pair_head = _head_block_ones(LANES, HG_DK)
    half = chunk // 2
    rows = lax.broadcasted_iota(jnp.int32, (half, HG_WIDTH), 0)
    zeros_half = jnp.zeros((half, HG_WIDTH), F32)

    order = range(tb // chunk - 1, -1, -1) if reverse else range(tb // chunk)
    for ci in order:
        sl = slice(ci * chunk, (ci + 1) * chunk)
        bc, qc, kc, vc = b[sl], qs[sl], kfull[sl], vfull[sl]
        b_end = bc[0:1] if reverse else bc[chunk - 1:chunk]
        qt = (qc * jnp.exp(bc)).astype(BF16)
        kh = (kc * jnp.exp(b_end - bc)).astype(BF16)
        vb = vc.astype(BF16)
        dec = jnp.exp(b_end)
        inter = []
        for p in range(HG_WIDTH // LANES):
            ln = slice(p * LANES, (p + 1) * LANES)
            st = st_sc[p]
            inter.append(lax.dot_general(qt[:, ln], st.astype(BF16), NT_DIMS, preferred_element_type=F32))
            kv = lax.dot_general(vb[:, ln], kh[:, ln], TN_DIMS, preferred_element_type=F32)
            st_sc[p] = jnp.where(pair_head, st * dec[:, ln] + kv, 0.0)
        o_c = jnp.concatenate(inter, axis=1)
        pieces = []
        for s in range(chunk):
            parts = []
            for hf in range(2):
                lo = hf * half
                live = (lo + half - 1 >= s) if not reverse else (lo <= s)
                if not live:
                    parts.append(zeros_half)
                    continue
                diff = bc[lo:lo + half] - bc[s:s + 1]
                all_kept = (lo + half - 1 <= s) if reverse else (lo >= s)
                if not all_kept:
                    keep = (rows + lo <= s) if reverse else (rows + lo >= s)
                    diff = jnp.where(keep, diff, NEG_BIG)
                e = jnp.exp(diff)
                parts.append(qc[lo:lo + half] * e * kc[s:s + 1])
            pieces.append(jnp.concatenate(parts, axis=0).astype(BF16))
        att = jnp.dot(jnp.concatenate(pieces, axis=0), bones, preferred_element_type=F32)
        for s in range(chunk):
            o_c = o_c + att[s * chunk:(s + 1) * chunk] * vc[s:s + 1]
        o_ref[sl, :] = o_c


def _hgrn(hg_in, lb_fwd, lb_bwd, batch, seq, tb, chunk):
    m = batch * seq
    nb = seq // tb
    fwd = lambda cidx: pl.BlockSpec((tb, HG_WIDTH), lambda b, j: (b * nb + j, cidx))
    bwd = lambda cidx: pl.BlockSpec((tb, HG_WIDTH), lambda b, j: (b * nb + nb - 1 - j, cidx))
    vec = pl.BlockSpec((1, HG_WIDTH), lambda b, j: (0, 0))
    state = pltpu.VMEM((HG_WIDTH // LANES, LANES, LANES), F32)
    return pl.pallas_call(
        functools.partial(_hgrn_kernel, tb=tb, chunk=chunk),
        grid=(batch, nb),
        in_specs=[fwd(0), fwd(1), fwd(3), vec, bwd(0), bwd(2), bwd(3), vec],
        out_specs=[fwd(0), bwd(0)],
        out_shape=[jax.ShapeDtypeStruct((m, HG_WIDTH), F32)] * 2,
        scratch_shapes=[state, state],
        compiler_params=_cparams(("parallel", "arbitrary")),
        name="hgrn",
    )(hg_in, hg_in, hg_in, lb_fwd, hg_in, hg_in, hg_in, lb_bwd)


def _rglru_kernel(xp_ref, x_ref, xn_ref, cw_ref, cb_ref, wg_ref, bg_ref, nsp_ref, h_ref, carry_sc,
                  *, tb, nb, reverse):
    j = pl.program_id(1)
    pos = (nb - 1 - j) if reverse else j

    @pl.when(j == 0)
    def _():
        carry_sc[...] = jnp.zeros(carry_sc.shape, F32)

    xp = jnp.where(pos > 0, xp_ref[...], 0.0)
    xn = jnp.where(pos < nb - 1, xn_ref[...], 0.0)
    xe = jnp.concatenate([xp, x_ref[...], xn], axis=0)
    cw = cw_ref[...]
    ne = tb + 2 * SUBLANES
    mid = slice(SUBLANES, SUBLANES + tb)
    tap = lambda k: pltpu.roll(xe, (-k) % ne, 0)[mid]
    xc = cw[0:1] * tap(-2) + cw[1:2] * tap(-1) + cw[2:3] * xe[mid] + cw[3:4] * tap(1) + cb_ref[...]
    gates = jnp.dot(xc.astype(BF16), wg_ref[...], preferred_element_type=F32) + bg_ref[...]
    rgate = jax.nn.sigmoid(gates[:, :RG_WIDTH])
    igate = jax.nn.sigmoid(gates[:, RG_WIDTH:])
    log_a = nsp_ref[...] * rgate
    a = jnp.exp(log_a)
    one_m_a2 = 1.0 - a * a
    u = one_m_a2 * lax.rsqrt(jnp.maximum(one_m_a2, TINY)) * (igate * xc)

    rows = lax.broadcasted_iota(jnp.int32, (tb, RG_WIDTH), 0) % SUBLANES
    d = 1
    while d < SUBLANES:
        if reverse:
            a_sh, u_sh, ok = pltpu.roll(a, tb - d, 0), pltpu.roll(u, tb - d, 0), rows < SUBLANES - d
        else:
            a_sh, u_sh, ok = pltpu.roll(a, d, 0), pltpu.roll(u, d, 0), rows >= d
        u = jnp.where(ok, a * u_sh + u, u)
        a = jnp.where(ok, a * a_sh, a)
        d *= 2
    ngroups = tb // SUBLANES
    carry = carry_sc[...]
    out = [None] * ngroups
    for gi in (range(ngroups - 1, -1, -1) if reverse else range(ngroups)):
        sl = slice(gi * SUBLANES, (gi + 1) * SUBLANES)
        hg = a[sl] * carry + u[sl]
        out[gi] = hg
        carry = hg[0:1] if reverse else hg[SUBLANES - 1:SUBLANES]
    h_ref[...] = jnp.concatenate(out, axis=0)
    carry_sc[...] = carry


def _rglru(rg_in, conv_w, conv_b, wg, bg, nsp, batch, seq, reverse, tb):
    m = batch * seq
    nb = seq // tb
    per8 = tb // SUBLANES
    last8 = m // SUBLANES - 1
    pos = (lambda j: nb - 1 - j) if reverse else (lambda j: j)
    blk = lambda b, j: b * nb + pos(j)
    kern = functools.partial(_rglru_kernel, tb=tb, nb=nb, reverse=reverse)
    return pl.pallas_call(
        kern,
        grid=(batch, nb),
        in_specs=[pl.BlockSpec((SUBLANES, RG_WIDTH), lambda b, j: (jnp.maximum(blk(b, j) * per8 - 1, 0), 0)),
                  pl.BlockSpec((tb, RG_WIDTH), lambda b, j: (blk(b, j), 0)),
                  pl.BlockSpec((SUBLANES, RG_WIDTH), lambda b, j: (jnp.minimum((blk(b, j) + 1) * per8, last8), 0)),
                  pl.BlockSpec(conv_w.shape, lambda b, j: (0, 0)),
                  pl.BlockSpec(conv_b.shape, lambda b, j: (0, 0)),
                  pl.BlockSpec(wg.shape, lambda b, j: (0, 0)),
                  pl.BlockSpec(bg.shape, lambda b, j: (0, 0)),
                  pl.BlockSpec(nsp.shape, lambda b, j: (0, 0))],
        out_specs=pl.BlockSpec((tb, RG_WIDTH), lambda b, j: (blk(b, j), 0)),
        out_shape=jax.ShapeDtypeStruct((m, RG_WIDTH), F32),
        scratch_shapes=[pltpu.VMEM((1, RG_WIDTH), F32)],
        compiler_params=_cparams(("parallel", "arbitrary")),
        name="rglru_bwd" if reverse else "rglru_fwd",
    )(rg_in, rg_in, rg_in, conv_w, conv_b, wg, bg, nsp)


def _mix_ca_kernel(x_ref, of_ref, ob_ref, hgg_ref, ng_ref, hf_ref, hb_ref, ry_ref, da_ref, wout_ref,
                   l1g_ref, l1b_ref, wq_ref, k_ref, v_ref, wo_ref, l2g_ref, l2b_ref, out_ref, *, alpha):
    o = of_ref[...] + ob_ref[...]
    bones = _head_block_ones(HG_WIDTH, HG_DK).astype(BF16)
    sq = o * o
    hi = sq.astype(BF16)
    lo = (sq - hi.astype(F32)).astype(BF16)
    ms = (jnp.dot(hi, bones, preferred_element_type=F32)
          + jnp.dot(lo, bones, preferred_element_type=F32)) * (1.0 / HG_DK)
    gate = hgg_ref[...]
    o_hg = o * lax.rsqrt(ms + RMS_EPS) * ng_ref[...] * (gate * jax.nn.sigmoid(gate))
    o_rg = (hf_ref[...] + hb_ref[...]) * _gelu_tanh(ry_ref[...])
    cat = jnp.concatenate([o_hg.astype(BF16), o_rg.astype(BF16)], axis=1)
    n_rec = HG_WIDTH + RG_WIDTH
    mix = (jnp.dot(cat, wout_ref[:n_rec, :], preferred_element_type=F32)
           + lax.dot_general(da_ref[...], wout_ref[n_rec:, :], TN_DIMS, preferred_element_type=F32))
    x = _layer_norm(alpha * x_ref[...] + mix, l1g_ref[...], l1b_ref[...])

    q = jnp.dot(x.astype(BF16), wq_ref[...], preferred_element_type=F32) * (CA_HEAD_DIM ** -0.5)
    outs = []
    for h in range(N_HEADS):
        sl = slice(h * CA_HEAD_DIM, (h + 1) * CA_HEAD_DIM)
        s = lax.dot_general(q[:, sl].astype(BF16), k_ref[:, sl], NT_DIMS, preferred_element_type=F32)
        p = jnp.exp(s - jnp.max(s, axis=1, keepdims=True))
        l = jnp.sum(p, axis=1, keepdims=True)
        o = jnp.dot(p.astype(BF16), v_ref[:, sl], preferred_element_type=F32) / l
        outs.append(o.astype(BF16))
    ca = jnp.dot(jnp.concatenate(outs, axis=1), wo_ref[...], preferred_element_type=F32)
    out_ref[...] = _layer_norm(alpha * x + ca, l2g_ref[...], l2b_ref[...])


def _mix_ca(x2, o_f, o_b, hg_in, norm_g, h_f, h_b, rg_in, o_da, w_out, ln1_g, ln1_b,
            wq, kmem, vmem, wo, ln2_g, ln2_b, batch, seq, n_mem, alpha, tm):
    m = x2.shape[0]
    nt = seq // tm
    row = lambda w, cidx=0: pl.BlockSpec((tm, w), lambda b, i: (b * nt + i, cidx))
    vec = lambda w: pl.BlockSpec((1, w), lambda b, i: (0, 0))
    mem_blk = pl.BlockSpec((n_mem, D_MODEL), lambda b, i: (b, 0))
    return pl.pallas_call(
        functools.partial(_mix_ca_kernel, alpha=alpha),
        grid=(batch, nt),
        in_specs=[row(D_MODEL), row(HG_WIDTH), row(HG_WIDTH), row(HG_WIDTH, 4), vec(HG_WIDTH),
                  row(RG_WIDTH), row(RG_WIDTH), row(RG_WIDTH, 1),
                  pl.BlockSpec((DA_WIDTH, tm), lambda b, i: (0, b * nt + i)),
                  _const_spec(w_out.shape), vec(D_MODEL), vec(D_MODEL),
                  _const_spec(wq.shape), mem_blk, mem_blk, _const_spec(wo.shape), vec(D_MODEL), vec(D_MODEL)],
        out_specs=row(D_MODEL),
        out_shape=jax.ShapeDtypeStruct((m, D_MODEL), F32),
        compiler_params=_cparams(("parallel", "parallel")),
        name="mix_cross_attn",
    )(x2, o_f, o_b, hg_in, norm_g, h_f, h_b, rg_in, o_da, w_out, ln1_g, ln1_b,
      wq, kmem, vmem, wo, ln2_g, ln2_b)


def _matmul_kernel(a_ref, w_ref, o_ref):
    o_ref[...] = jnp.dot(a_ref[...].astype(BF16), w_ref[...], preferred_element_type=F32).astype(o_ref.dtype)


def _mem_proj(mem2, w):
    m, n = mem2.shape[0], w.shape[1]
    return pl.pallas_call(
        _matmul_kernel,
        grid=(1,),
        in_specs=[pl.BlockSpec(mem2.shape, lambda i: (0, 0)), pl.BlockSpec(w.shape, lambda i: (0, 0))],
        out_specs=pl.BlockSpec((m, n), lambda i: (0, 0)),
        out_shape=jax.ShapeDtypeStruct((m, n), BF16),
        compiler_params=_cparams(("arbitrary",)),
        name="mem_proj",
    )(mem2, w)


def _ffn_kernel(xp_ref, x_ref, xn_ref, wu_ref, cw_ref, cb_ref, wd_ref, lg_ref, lb_ref, out_ref,
                *, tm, nt, alpha):
    pos = pl.program_id(1)
    x = x_ref[...]
    xp = jnp.where(pos > 0, xp_ref[...], 0.0)
    xn = jnp.where(pos < nt - 1, xn_ref[...], 0.0)
    xe = jnp.concatenate([xp, x, xn], axis=0).astype(BF16)
    gate = jnp.dot(xe, wu_ref[:, :D_FF], preferred_element_type=F32)
    cw = cw_ref[...]
    ne = tm + 2 * SUBLANES
    mid = slice(SUBLANES, SUBLANES + tm)
    gc = (cw[0:1] * pltpu.roll(gate, 1, 0)[mid] + cw[1:2] * gate[mid]
          + cw[2:3] * pltpu.roll(gate, ne - 1, 0)[mid] + cb_ref[...])
    val = jnp.dot(x.astype(BF16), wu_ref[:, D_FF:], preferred_element_type=F32)
    hid = (_gelu_tanh(gc) * val).astype(BF16)
    ff = jnp.dot(hid, wd_ref[...], preferred_element_type=F32)
    out_ref[...] = _layer_norm(alpha * x + ff, lg_ref[...], lb_ref[...])


def _ffn(x2, wu, conv_w, conv_b, wd, ln_g, ln_b, batch, seq, alpha, tm):
    m = x2.shape[0]
    nt = seq // tm
    per8 = tm // SUBLANES
    last8 = m // SUBLANES - 1
    blk = lambda b, i: b * nt + i
    vec = pl.BlockSpec((1, D_MODEL), lambda b, i: (0, 0))
    return pl.pallas_call(
        functools.partial(_ffn_kernel, tm=tm, nt=nt, alpha=alpha),
        grid=(batch, nt),
        in_specs=[pl.BlockSpec((SUBLANES, D_MODEL), lambda b, i: (jnp.maximum(blk(b, i) * per8 - 1, 0), 0)),
                  pl.BlockSpec((tm, D_MODEL), lambda b, i: (blk(b, i), 0)),
                  pl.BlockSpec((SUBLANES, D_MODEL), lambda b, i: (jnp.minimum((blk(b, i) + 1) * per8, last8), 0)),
                  _const_spec(wu.shape),
                  pl.BlockSpec(conv_w.shape, lambda b, i: (0, 0)),
                  pl.BlockSpec(conv_b.shape, lambda b, i: (0, 0)),
                  _const_spec(wd.shape), vec, vec],
        out_specs=pl.BlockSpec((tm, D_MODEL), lambda b, i: (blk(b, i), 0)),
        out_shape=jax.ShapeDtypeStruct((m, D_MODEL), F32),
        compiler_params=_cparams(("parallel", "parallel")),
        name="conv_glu",
    )(x2, x2, x2, wu, conv_w, conv_b, wd, ln_g, ln_b)


def _block_diag(w):
    n, blk, _ = w.shape
    eye = jnp.eye(n, dtype=w.dtype)
    return jnp.einsum("nij,nm->nimj", w, eye).reshape(n * blk, n * blk)


def _pick(seq, pref):
    t = min(seq, pref)
    assert seq % t == 0, (seq, pref)
    return t


def kernel(x, mem, positions, w_in, hg_lower_bounds, hg_norm_g, rg_conv_w, rg_conv_b, rg_wa, rg_ba, rg_wx, rg_bx, rg_lambda, da_lambda, da_subln_g, w_out, ln1_g, ln1_b, ca_wq, ca_wk, ca_wv, ca_wo, ln2_g, ln2_b, ffn_w_up, ffn_conv_w, ffn_conv_b, ffn_w_down, ln3_g, ln3_b):
    batch, seq, _ = x.shape
    n_mem = mem.shape[1]
    depth = w_in.shape[0]
    alpha = (2 * depth) ** 0.25
    m = batch * seq

    tm_proj = _pick(seq, 512)
    tm_ffn = _pick(seq, 256)
    tq = _pick(seq, 256)
    tb_hg = _pick(seq, 128)
    tb_rg = _pick(seq, 512)

    inv_freq = 1.0 / (ROPE_THETA ** (jnp.arange(0, DA_QK_DIM, 2, dtype=F32) / DA_QK_DIM))
    ang = positions.astype(F32).reshape(m, 1) * inv_freq
    cos, sin = jnp.cos(ang), jnp.sin(ang)
    cos_t = jnp.tile(cos, (1, 4))
    sin_t = jnp.tile(jnp.concatenate([-sin, sin], axis=1), (1, 2))

    lb_soft = jax.nn.softmax(hg_lower_bounds.astype(F32), axis=0)
    lb_all = jnp.cumsum(lb_soft, axis=0) - lb_soft[0:1]

    x2 = x.reshape(m, D_MODEL)
    mem2 = mem.reshape(batch * n_mem, D_MODEL)
    row = lambda v: v.reshape(1, -1).astype(F32)

    for layer in range(depth):
        hg_in, rg_in, qt, kk, vt = _inproj(x2, w_in[layer].astype(BF16), cos_t, sin_t, tm_proj)

        lam_init = 0.8 - 0.6 * math.exp(-0.3 * layer)
        lp = da_lambda[layer].astype(F32)
        lam = jnp.exp(jnp.sum(lp[0] * lp[1])) - jnp.exp(jnp.sum(lp[2] * lp[3])) + lam_init
        gain = jnp.broadcast_to(da_subln_g[layer].astype(F32)[:, None], (DA_V_DIM, LANES))
        o_da = _diff_attention(jnp.full((1, LANES), lam, F32), qt, kk, vt, gain,
                               batch, seq, 1.0 - lam_init, tq, tm_proj)

        o_f, o_b = _hgrn(hg_in, row(lb_all[layer, 0]), row(lb_all[layer, 1]), batch, seq, tb_hg, 16)

        nsp = -RG_C * jax.nn.softplus(-rg_lambda[layer].astype(F32))
        h_dirs = []
        for d in range(2):
            wg = jnp.concatenate([_block_diag(rg_wa[layer, d]), _block_diag(rg_wx[layer, d])], axis=1).astype(BF16)
            bg = jnp.concatenate([rg_ba[layer, d], rg_bx[layer, d]]).reshape(1, -1).astype(F32)
            h_dirs.append(_rglru(rg_in, rg_conv_w[layer].astype(F32), row(rg_conv_b[layer]), wg, bg,
                                 row(nsp[d]), batch, seq, d == 1, tb_rg))

        kmem = _mem_proj(mem2, ca_wk[layer].astype(BF16))
        vmem = _mem_proj(mem2, ca_wv[layer].astype(BF16))
        x2 = _mix_ca(x2, o_f, o_b, hg_in, row(hg_norm_g[layer]), h_dirs[0], h_dirs[1], rg_in, o_da,
                     w_out[layer].astype(BF16), row(ln1_g[layer]), row(ln1_b[layer]),
                     ca_wq[layer].astype(BF16), kmem, vmem, ca_wo[layer].astype(BF16),
                     row(ln2_g[layer]), row(ln2_b[layer]), batch, seq, n_mem, alpha, tm_proj)

        x2 = _ffn(x2, ffn_w_up[layer].astype(BF16), ffn_conv_w[layer].astype(F32), row(ffn_conv_b[layer]),
                  ffn_w_down[layer].astype(BF16), row(ln3_g[layer]), row(ln3_b[layer]), batch, seq, alpha, tm_ffn)

    return x2.reshape(batch, seq, D_MODEL)
```

```python
import functools
import math

import jax
import jax.numpy as jnp
from jax import lax
from jax.experimental import pallas as pl
from jax.experimental.pallas import tpu as pltpu

F32 = jnp.float32
BF16 = jnp.bfloat16

D_MODEL = 1024
N_HEADS = 4
HG_WIDTH = 256
HG_DK = 64
RG_WIDTH = 256
RG_BLOCK = 64
RG_C = 8.0
DA_WIDTH = 512
DA_V_DIM = 128
DA_QK_DIM = 64
ROPE_THETA = 10000.0
CA_HEAD_DIM = 256
D_FF = 2816
LN_EPS = 1e-5
RMS_EPS = 1e-6

LANES = 128
SUBLANES = 8
VMEM_LIMIT = 56 * 1024 * 1024

NT_DIMS = (((1,), (1,)), ((), ()))
TN_DIMS = (((0,), (0,)), ((), ()))
NEG_BIG = -1e30
TINY = 1e-30
LOG2_E = math.log2(math.e)
ATTN_UNROLL = 16
BF16_ROWS = 16
VT_ROWS = DA_V_DIM + BF16_ROWS


def _cparams(sem, flags=None):
    return pltpu.CompilerParams(dimension_semantics=sem, vmem_limit_bytes=VMEM_LIMIT, flags=flags)


def _layer_spec(w, layer):
    return pl.BlockSpec((None,) + w.shape[1:], lambda *_: (layer, 0, 0), pipeline_mode=pl.Buffered(1))


def _layer_norm(y, g, b):
    mu = jnp.mean(y, axis=-1, keepdims=True)
    d = y - mu
    var = jnp.mean(d * d, axis=-1, keepdims=True)
    return d * lax.rsqrt(var + LN_EPS) * g + b


def _gelu_tanh(x):
    return 0.5 * x * (1.0 + jnp.tanh(math.sqrt(2.0 / math.pi) * (x + 0.044715 * (x * x * x))))


def _head_block_ones(n, blk):
    r = lax.broadcasted_iota(jnp.int32, (n, n), 0) // blk
    c = lax.broadcasted_iota(jnp.int32, (n, n), 1) // blk
    return r == c


def _inproj_kernel(x_ref, w_ref, cos_ref, sin_ref, hg_ref, rg_ref, qm_ref, k_ref, v_ref):
    xb = x_ref[...].astype(BF16)
    o_rg = 5 * HG_WIDTH
    o_q = o_rg + 2 * RG_WIDTH
    o_k, o_v = o_q + DA_WIDTH, o_q + 2 * DA_WIDTH
    proj = lambda lo, hi: jnp.dot(xb, w_ref[:, lo:hi], preferred_element_type=F32)
    hg_ref[...] = proj(0, o_rg)
    rg_ref[...] = proj(o_rg, o_q)

    reps = DA_WIDTH // LANES
    c = jnp.concatenate([cos_ref[...]] * reps, axis=1)
    sgn_sin = jnp.concatenate([sin_ref[...]] * reps, axis=1)
    half = DA_QK_DIM // 2
    lane = lax.broadcasted_iota(jnp.int32, c.shape, 1)
    first_half = (lane % DA_QK_DIM) < half

    def rot(t):
        partner = jnp.where(first_half, pltpu.roll(t, DA_WIDTH - half, 1), pltpu.roll(t, half, 1))
        return t * c + partner * sgn_sin

    q = rot(proj(o_q, o_k)) * (DA_QK_DIM ** -0.5 * LOG2_E)
    qt = q.T
    sub = lax.broadcasted_iota(jnp.int32, qt.shape, 0)
    first = (sub % DA_V_DIM) < DA_QK_DIM
    qm_ref[0] = jnp.where(first, qt, 0.0).astype(BF16)
    qm_ref[1] = jnp.where(first, 0.0, qt).astype(BF16)
    k_ref[...] = rot(proj(o_k, o_v)).astype(BF16)
    vt = proj(o_v, o_v + DA_WIDTH).T.astype(BF16)
    ones = jnp.ones((VT_ROWS - DA_V_DIM, vt.shape[1]), BF16)
    for h in range(N_HEADS):
        v_ref[0, h * VT_ROWS:h * VT_ROWS + DA_V_DIM, :] = vt[h * DA_V_DIM:(h + 1) * DA_V_DIM]
        v_ref[0, h * VT_ROWS + DA_V_DIM:(h + 1) * VT_ROWS, :] = ones


def _inproj(x2, w, layer, cos_t, sin_t, tm):
    m = x2.shape[0]
    row = lambda w: pl.BlockSpec((tm, w), lambda i: (i, 0))
    return pl.pallas_call(
        _inproj_kernel,
        grid=(m // tm,),
        in_specs=[row(D_MODEL), _layer_spec(w, layer), row(LANES), row(LANES)],
        out_specs=[row(5 * HG_WIDTH), row(2 * RG_WIDTH),
                   pl.BlockSpec((2, DA_WIDTH, tm), lambda i: (0, 0, i)), row(DA_WIDTH),
                   pl.BlockSpec((1, N_HEADS * VT_ROWS, tm), lambda i: (i, 0, 0))],
        out_shape=[jax.ShapeDtypeStruct((m, 5 * HG_WIDTH), F32),
                   jax.ShapeDtypeStruct((m, 2 * RG_WIDTH), F32),
                   jax.ShapeDtypeStruct((2, DA_WIDTH, m), BF16),
                   jax.ShapeDtypeStruct((m, DA_WIDTH), BF16),
                   jax.ShapeDtypeStruct((m // tm, N_HEADS * VT_ROWS, tm), BF16)],
        compiler_params=_cparams(("parallel",)),
        name="inproj",
    )(x2, w, cos_t, sin_t)


def _attn_kernel(lam_ref, qt_ref, qn_ref, k_ref, vt_ref, g_ref, o_ref, q_sc, s_sc, acc_sc,
                 *, tq, tk, nk, unroll, out_scale):
    q_sc[0] = jnp.concatenate([qt_ref[0], qt_ref[1]], axis=1)
    q_sc[1] = jnp.concatenate([qn_ref[0], qn_ref[1]], axis=1)
    acc_sc[...] = jnp.zeros(acc_sc.shape, F32)

    def scores(idx, slot, tile):
        kc = k_ref[pl.ds(pl.multiple_of(idx * tk, tk), tk), :]
        s_sc[slot] = jnp.dot(kc, q_sc[tile], preferred_element_type=F32)

    def absorb(idx, slot, m_prev):
        s = s_sc[slot]
        m_next = jnp.maximum(m_prev, jnp.max(s, axis=0, keepdims=True))
        p = jnp.exp2(s - m_next).astype(BF16)
        pv = jnp.dot(vt_ref[idx], p, preferred_element_type=F32)
        acc_sc[...] = jnp.exp2(m_prev - m_next) * acc_sc[...] + pv
        return m_next

    @pl.when(pl.program_id(2) == 0)
    def _():
        scores(0, 0, 0)

    def body(t, m_run):
        for u in range(unroll):
            c = unroll * t + u
            wrap = jnp.asarray(c + 1 == nk).astype(jnp.int32)
            scores((c + 1) * (1 - wrap), (u + 1) % 2, wrap)
            m_run = absorb(c, u % 2, m_run)
        return m_run

    lax.fori_loop(0, nk // unroll, body, jnp.full((1, 2 * tq), -jnp.inf, F32))
    o = acc_sc[:DA_V_DIM, :] / acc_sc[DA_V_DIM:DA_V_DIM + 1, :]
    d = o[:, :tq] - lam_ref[...][:, :1] * o[:, tq:]
    ms = jnp.mean(d * d, axis=0, keepdims=True)
    gain = jnp.concatenate([g_ref[...]] * (tq // LANES), axis=1)
    o_ref[...] = (d * lax.rsqrt(ms + RMS_EPS) * gain * out_scale).astype(o_ref.dtype)


def _diff_attention(lam_row, qt, kk, vt, subln_g, batch, seq, out_scale, tq, tk):
    m = batch * seq
    nq = seq // tq
    nk = seq // tk
    unroll = math.gcd(nk, ATTN_UNROLL)
    assert unroll % 2 == 0, nk
    kern = functools.partial(_attn_kernel, tq=tq, tk=tk, nk=nk, unroll=unroll, out_scale=out_scale)
    return pl.pallas_call(
        kern,
        grid=(batch, N_HEADS, nq),
        in_specs=[pl.BlockSpec((1, LANES), lambda b, h, i: (0, 0)),
                  pl.BlockSpec((2, DA_V_DIM, tq), lambda b, h, i: (0, h, b * nq + i)),
                  pl.BlockSpec((2, DA_V_DIM, tq), lambda b, h, i: (0, h, b * nq + jnp.minimum(i + 1, nq - 1))),
                  pl.BlockSpec((seq, DA_V_DIM), lambda b, h, i: (b, h)),
                  pl.BlockSpec((nk, VT_ROWS, tk), lambda b, h, i: (b, h, 0)),
                  pl.BlockSpec((DA_V_DIM, LANES), lambda b, h, i: (0, 0))],
        out_specs=pl.BlockSpec((DA_V_DIM, tq), lambda b, h, i: (h, b * nq + i)),
        out_shape=jax.ShapeDtypeStruct((DA_WIDTH, m), BF16),
        scratch_shapes=[pltpu.VMEM((2, DA_V_DIM, 2 * tq), BF16), pltpu.VMEM((2, tk, 2 * tq), F32),
                        pltpu.VMEM((VT_ROWS, 2 * tq), F32)],
        compiler_params=_cparams(("parallel", "parallel", "arbitrary")),
        name="diff_attn",
    )(lam_row, qt, qt, kk, vt, subln_g)


def _hgrn_kernel(qf_ref, ff_ref, vf_ref, lbf_ref, qb_ref, fb_ref, vb_ref, lbb_ref, of_ref, ob_ref,
                 stf_sc, stb_sc, *, tb, chunk):
    @pl.when(pl.program_id(1) == 0)
    def _():
        stf_sc[...] = jnp.zeros(stf_sc.shape, F32)
        stb_sc[...] = jnp.zeros(stb_sc.shape, F32)

    _hgrn_direction(qf_ref, ff_ref, vf_ref, lbf_ref, of_ref, stf_sc, tb, chunk, False)
    _hgrn_direction(qb_ref, fb_ref, vb_ref, lbb_ref, ob_ref, stb_sc, tb, chunk, True)


def _hgrn_direction(q_ref, f_ref, v_ref, lb_ref, o_ref, st_sc, tb, chunk, reverse):
    lb = lb_ref[...]
    z = f_ref[...]
    qraw = q_ref[...]
    qs = qraw * jax.nn.sigmoid(qraw)
    g = jnp.log(lb + (1.0 - lb) * jax.nn.sigmoid(z))
    kfull = (1.0 - lb) * jax.nn.sigmoid(-z)
    vfull = v_ref[...]

    r = lax.broadcasted_iota(jnp.int32, (tb, tb), 0)
    c = lax.broadcasted_iota(jnp.int32, (tb, tb), 1)
    tri = ((r // chunk) == (c // chunk)) & ((c >= r) if reverse else (c <= r))
    b = jnp.dot(tri.astype(F32), g, precision=lax.Precision.HIGHEST, preferred_element_type=F32)

    bones = _head_block_ones(HG_WIDTH, HG_DK).astype(BF16)
    pair_head = _head_block_ones(LANES, HG_DK)
    half = chunk // 2
    rows = lax.broadcasted_iota(jnp.int32, (half, HG_WIDTH), 0)
    zeros_half = jnp.zeros((half, HG_WIDTH), F32)

    order = range(tb // chunk - 1, -1, -1) if reverse else range(tb // chunk)
    for ci in order:
        sl = slice(ci * chunk, (ci + 1) * chunk)
        bc, qc, kc, vc = b[sl], qs[sl], kfull[sl], vfull[sl]
        b_end = bc[0:1] if reverse else bc[chunk - 1:chunk]
        qt = (qc * jnp.exp(bc)).astype(BF16)
        kh = (kc * jnp.exp(b_end - bc)).astype(BF16)
        vb = vc.astype(BF16)
        dec = jnp.exp(b_end)
        inter = []
        for p in range(HG_WIDTH // LANES):
            ln = slice(p * LANES, (p + 1) * LANES)
            st = st_sc[p]
            inter.append(lax.dot_general(qt[:, ln], st.astype(BF16), NT_DIMS, preferred_element_type=F32))
            kv = lax.dot_general(vb[:, ln], kh[:, ln], TN_DIMS, preferred_element_type=F32)
            st_sc[p] = jnp.where(pair_head, st * dec[:, ln] + kv, 0.0)
        o_c = jnp.concatenate(inter, axis=1)
        pieces = []
        for s in range(chunk):
            parts = []
            for hf in range(2):
                lo = hf * half
                live = (lo + half - 1 >= s) if not reverse else (lo <= s)
                if not live:
                    parts.append(zeros_half)
                    continue
                diff = bc[lo:lo + half] - bc[s:s + 1]
                all_kept = (lo + half - 1 <= s) if reverse else (lo >= s)
                if not all_kept:
                    keep = (rows + lo <= s) if reverse else (rows + lo >= s)
                    diff = jnp.where(keep, diff, NEG_BIG)
                e = jnp.exp(diff)
                parts.append(qc[lo:lo + half] * e * kc[s:s + 1])
            pieces.append(jnp.concatenate(parts, axis=0).astype(BF16))
        att = jnp.dot(jnp.concatenate(pieces, axis=0), bones, preferred_element_type=F32)
        for s in range(chunk):
            o_c = o_c + att[s * chunk:(s + 1) * chunk] * vc[s:s + 1]
        o_ref[sl, :] = o_c


def _hgrn(hg_in, lb_fwd, lb_bwd, batch, seq, tb, chunk):
    m = batch * seq
    nb = seq // tb
    fwd = lambda cidx: pl.BlockSpec((tb, HG_WIDTH), lambda b, j: (b * nb + j, cidx))
    bwd = lambda cidx: pl.BlockSpec((tb, HG_WIDTH), lambda b, j: (b * nb + nb - 1 - j, cidx))
    vec = pl.BlockSpec((1, HG_WIDTH), lambda b, j: (0, 0))
    state = pltpu.VMEM((HG_WIDTH // LANES, LANES, LANES), F32)
    return pl.pallas_call(
        functools.partial(_hgrn_kernel, tb=tb, chunk=chunk),
        grid=(batch, nb),
        in_specs=[fwd(0), fwd(1), fwd(3), vec, bwd(0), bwd(2), bwd(3), vec],
        out_specs=[fwd(0), bwd(0)],
        out_shape=[jax.ShapeDtypeStruct((m, HG_WIDTH), F32)] * 2,
        scratch_shapes=[state, state],
        compiler_params=_cparams(("parallel", "arbitrary")),
        name="hgrn",
    )(hg_in, hg_in, hg_in, lb_fwd, hg_in, hg_in, hg_in, lb_bwd)


def _rglru_kernel(xp_ref, x_ref, xn_ref, cw_ref, cb_ref, wg_ref, bg_ref, nsp_ref, h_ref, carry_sc,
                  *, tb, nb, reverse):
    j = pl.program_id(1)
    pos = (nb - 1 - j) if reverse else j

    @pl.when(j == 0)
    def _():
        carry_sc[...] = jnp.zeros(carry_sc.shape, F32)

    xp = jnp.where(pos > 0, xp_ref[...], 0.0)
    xn = jnp.where(pos < nb - 1, xn_ref[...], 0.0)
    xe = jnp.concatenate([xp, x_ref[...], xn], axis=0)
    cw = cw_ref[...]
    ne = tb + 2 * SUBLANES
    mid = slice(SUBLANES, SUBLANES + tb)
    tap = lambda k: pltpu.roll(xe, (-k) % ne, 0)[mid]
    xc = cw[0:1] * tap(-2) + cw[1:2] * tap(-1) + cw[2:3] * xe[mid] + cw[3:4] * tap(1) + cb_ref[...]
    gates = jnp.dot(xc.astype(BF16), wg_ref[...], preferred_element_type=F32) + bg_ref[...]
    rgate = jax.nn.sigmoid(gates[:, :RG_WIDTH])
    igate = jax.nn.sigmoid(gates[:, RG_WIDTH:])
    log_a = nsp_ref[...] * rgate
    a = jnp.exp(log_a)
    one_m_a2 = 1.0 - a * a
    u = one_m_a2 * lax.rsqrt(jnp.maximum(one_m_a2, TINY)) * (igate * xc)

    rows = lax.broadcasted_iota(jnp.int32, (tb, RG_WIDTH), 0) % SUBLANES
    d = 1
    while d < SUBLANES:
        if reverse:
            a_sh, u_sh, ok = pltpu.roll(a, tb - d, 0), pltpu.roll(u, tb - d, 0), rows < SUBLANES - d
        else:
            a_sh, u_sh, ok = pltpu.roll(a, d, 0), pltpu.roll(u, d, 0), rows >= d
        u = jnp.where(ok, a * u_sh + u, u)
        a = jnp.where(ok, a * a_sh, a)
        d *= 2
    ngroups = tb // SUBLANES
    carry = carry_sc[...]
    out = [None] * ngroups
    for gi in (range(ngroups - 1, -1, -1) if reverse else range(ngroups)):
        sl = slice(gi * SUBLANES, (gi + 1) * SUBLANES)
        hg = a[sl] * carry + u[sl]
        out[gi] = hg
        carry = hg[0:1] if reverse else hg[SUBLANES - 1:SUBLANES]
    h_ref[...] = jnp.concatenate(out, axis=0)
    carry_sc[...] = carry


def _rglru(rg_in, conv_w, conv_b, wg, bg, nsp, batch, seq, reverse, tb):
    m = batch * seq
    nb = seq // tb
    per8 = tb // SUBLANES
    last8 = m // SUBLANES - 1
    pos = (lambda j: nb - 1 - j) if reverse else (lambda j: j)
    blk = lambda b, j: b * nb + pos(j)
    kern = functools.partial(_rglru_kernel, tb=tb, nb=nb, reverse=reverse)
    return pl.pallas_call(
        kern,
        grid=(batch, nb),
        in_specs=[pl.BlockSpec((SUBLANES, RG_WIDTH), lambda b, j: (jnp.maximum(blk(b, j) * per8 - 1, 0), 0)),
                  pl.BlockSpec((tb, RG_WIDTH), lambda b, j: (blk(b, j), 0)),
                  pl.BlockSpec((SUBLANES, RG_WIDTH), lambda b, j: (jnp.minimum((blk(b, j) + 1) * per8, last8), 0)),
                  pl.BlockSpec(conv_w.shape, lambda b, j: (0, 0)),
                  pl.BlockSpec(conv_b.shape, lambda b, j: (0, 0)),
                  pl.BlockSpec(wg.shape, lambda b, j: (0, 0)),
                  pl.BlockSpec(bg.shape, lambda b, j: (0, 0)),
                  pl.BlockSpec(nsp.shape, lambda b, j: (0, 0))],
        out_specs=pl.BlockSpec((tb, RG_WIDTH), lambda b, j: (blk(b, j), 0)),
        out_shape=jax.ShapeDtypeStruct((m, RG_WIDTH), F32),
        scratch_shapes=[pltpu.VMEM((1, RG_WIDTH), F32)],
        compiler_params=_cparams(("parallel", "arbitrary")),
        name="rglru_bwd" if reverse else "rglru_fwd",
    )(rg_in, rg_in, rg_in, conv_w, conv_b, wg, bg, nsp)


def _mix_ca_kernel(x_ref, of_ref, ob_ref, hgg_ref, ng_ref, hf_ref, hb_ref, ry_ref, da_ref, wout_ref,
                   l1g_ref, l1b_ref, wq_ref, k_ref, v_ref, wo_ref, l2g_ref, l2b_ref, out_ref, *, alpha):
    o = of_ref[...] + ob_ref[...]
    bones = _head_block_ones(HG_WIDTH, HG_DK).astype(BF16)
    sq = o * o
    hi = sq.astype(BF16)
    lo = (sq - hi.astype(F32)).astype(BF16)
    ms = (jnp.dot(hi, bones, preferred_element_type=F32)
          + jnp.dot(lo, bones, preferred_element_type=F32)) * (1.0 / HG_DK)
    gate = hgg_ref[...]
    o_hg = o * lax.rsqrt(ms + RMS_EPS) * ng_ref[...] * (gate * jax.nn.sigmoid(gate))
    o_rg = (hf_ref[...] + hb_ref[...]) * _gelu_tanh(ry_ref[...])
    cat = jnp.concatenate([o_hg.astype(BF16), o_rg.astype(BF16)], axis=1)
    n_rec = HG_WIDTH + RG_WIDTH
    mix = (jnp.dot(cat, wout_ref[:n_rec, :], preferred_element_type=F32)
           + lax.dot_general(da_ref[...], wout_ref[n_rec:, :], TN_DIMS, preferred_element_type=F32))
    x = _layer_norm(alpha * x_ref[...] + mix, l1g_ref[...], l1b_ref[...])

    q = jnp.dot(x.astype(BF16), wq_ref[...], preferred_element_type=F32) * (CA_HEAD_DIM ** -0.5)
    outs = []
    for h in range(N_HEADS):
        sl = slice(h * CA_HEAD_DIM, (h + 1) * CA_HEAD_DIM)
        s = lax.dot_general(q[:, sl].astype(BF16), k_ref[:, sl], NT_DIMS, preferred_element_type=F32)
        p = jnp.exp(s - jnp.max(s, axis=1, keepdims=True))
        l = jnp.sum(p, axis=1, keepdims=True)
        o = jnp.dot(p.astype(BF16), v_ref[:, sl], preferred_element_type=F32) / l
        outs.append(o.astype(BF16))
    ca = jnp.dot(jnp.concatenate(outs, axis=1), wo_ref[...], preferred_element_type=F32)
    out_ref[...] = _layer_norm(alpha * x + ca, l2g_ref[...], l2b_ref[...])


def _mix_ca(x2, o_f, o_b, hg_in, norm_g, h_f, h_b, rg_in, o_da, w_out, ln1_g, ln1_b,
            wq, kmem, vmem, wo, ln2_g, ln2_b, layer, batch, seq, n_mem, alpha, tm):
    m = x2.shape[0]
    nt = seq // tm
    row = lambda w, cidx=0: pl.BlockSpec((tm, w), lambda b, i: (b * nt + i, cidx))
    vec = lambda w: pl.BlockSpec((1, w), lambda b, i: (0, 0))
    mem_blk = pl.BlockSpec((n_mem, D_MODEL), lambda b, i: (b, 0))
    return pl.pallas_call(
        functools.partial(_mix_ca_kernel, alpha=alpha),
        grid=(batch, nt),
        in_specs=[row(D_MODEL), row(HG_WIDTH), row(HG_WIDTH), row(HG_WIDTH, 4), vec(HG_WIDTH),
                  row(RG_WIDTH), row(RG_WIDTH), row(RG_WIDTH, 1),
                  pl.BlockSpec((DA_WIDTH, tm), lambda b, i: (0, b * nt + i)),
                  _layer_spec(w_out, layer), vec(D_MODEL), vec(D_MODEL),
                  _layer_spec(wq, layer), mem_blk, mem_blk, _layer_spec(wo, layer), vec(D_MODEL), vec(D_MODEL)],
        out_specs=row(D_MODEL),
        out_shape=jax.ShapeDtypeStruct((m, D_MODEL), F32),
        compiler_params=_cparams(("parallel", "parallel")),
        name="mix_cross_attn",
    )(x2, o_f, o_b, hg_in, norm_g, h_f, h_b, rg_in, o_da, w_out, ln1_g, ln1_b,
      wq, kmem, vmem, wo, ln2_g, ln2_b)


def _matmul_kernel(a_ref, w_ref, o_ref):
    o_ref[...] = jnp.dot(a_ref[...].astype(BF16), w_ref[...], preferred_element_type=F32).astype(o_ref.dtype)


def _mem_proj(mem2, w, layer):
    m, n = mem2.shape[0], w.shape[2]
    return pl.pallas_call(
        _matmul_kernel,
        grid=(1,),
        in_specs=[pl.BlockSpec(mem2.shape, lambda i: (0, 0)), _layer_spec(w, layer)],
        out_specs=pl.BlockSpec((m, n), lambda i: (0, 0)),
        out_shape=jax.ShapeDtypeStruct((m, n), BF16),
        compiler_params=_cparams(("arbitrary",)),
        name="mem_proj",
    )(mem2, w)


def _ffn_kernel(xp_ref, x_ref, xn_ref, wu_ref, cw_ref, cb_ref, wd_ref, lg_ref, lb_ref, out_ref,
                *, tm, nt, alpha):
    pos = pl.program_id(1)
    x = x_ref[...]
    xp = jnp.where(pos > 0, xp_ref[...], 0.0)
    xn = jnp.where(pos < nt - 1, xn_ref[...], 0.0)
    xe = jnp.concatenate([xp, x, xn], axis=0).astype(BF16)
    gate = jnp.dot(xe, wu_ref[:, :D_FF], preferred_element_type=F32)
    cw = cw_ref[...]
    ne = tm + 2 * SUBLANES
    mid = slice(SUBLANES, SUBLANES + tm)
    gc = (cw[0:1] * pltpu.roll(gate, 1, 0)[mid] + cw[1:2] * gate[mid]
          + cw[2:3] * pltpu.roll(gate, ne - 1, 0)[mid] + cb_ref[...])
    val = jnp.dot(x.astype(BF16), wu_ref[:, D_FF:], preferred_element_type=F32)
    hid = (_gelu_tanh(gc) * val).astype(BF16)
    ff = jnp.dot(hid, wd_ref[...], preferred_element_type=F32)
    out_ref[...] = _layer_norm(alpha * x + ff, lg_ref[...], lb_ref[...])


def _ffn(x2, wu, conv_w, conv_b, wd, ln_g, ln_b, layer, batch, seq, alpha, tm):
    m = x2.shape[0]
    nt = seq // tm
    per8 = tm // SUBLANES
    last8 = m // SUBLANES - 1
    blk = lambda b, i: b * nt + i
    vec = pl.BlockSpec((1, D_MODEL), lambda b, i: (0, 0))
    return pl.pallas_call(
        functools.partial(_ffn_kernel, tm=tm, nt=nt, alpha=alpha),
        grid=(batch, nt),
        in_specs=[pl.BlockSpec((SUBLANES, D_MODEL), lambda b, i: (jnp.maximum(blk(b, i) * per8 - 1, 0), 0)),
                  pl.BlockSpec((tm, D_MODEL), lambda b, i: (blk(b, i), 0)),
                  pl.BlockSpec((SUBLANES, D_MODEL), lambda b, i: (jnp.minimum((blk(b, i) + 1) * per8, last8), 0)),
                  _layer_spec(wu, layer),
                  pl.BlockSpec(conv_w.shape, lambda b, i: (0, 0)),
                  pl.BlockSpec(conv_b.shape, lambda b, i: (0, 0)),
                  _layer_spec(wd, layer), vec, vec],
        out_specs=pl.BlockSpec((tm, D_MODEL), lambda b, i: (blk(b, i), 0)),
        out_shape=jax.ShapeDtypeStruct((m, D_MODEL), F32),
        compiler_params=_cparams(("parallel", "parallel")),
        name="conv_glu",
    )(x2, x2, x2, wu, conv_w, conv_b, wd, ln_g, ln_b)


def _block_diag(w):
    n, blk, _ = w.shape
    eye = jnp.eye(n, dtype=w.dtype)
    return jnp.einsum("nij,nm->nimj", w, eye).reshape(n * blk, n * blk)


def _pick(seq, pref):
    t = min(seq, pref)
    assert seq % t == 0, (seq, pref)
    return t


def kernel(x, mem, positions, w_in, hg_lower_bounds, hg_norm_g, rg_conv_w, rg_conv_b, rg_wa, rg_ba, rg_wx, rg_bx, rg_lambda, da_lambda, da_subln_g, w_out, ln1_g, ln1_b, ca_wq, ca_wk, ca_wv, ca_wo, ln2_g, ln2_b, ffn_w_up, ffn_conv_w, ffn_conv_b, ffn_w_down, ln3_g, ln3_b):
    batch, seq, _ = x.shape
    n_mem = mem.shape[1]
    depth = w_in.shape[0]
    alpha = (2 * depth) ** 0.25
    m = batch * seq

    tm_proj = _pick(seq, 512)
    tm_ffn = _pick(seq, 256)
    tq = _pick(seq, 256)
    tb_hg = _pick(seq, 128)
    tb_rg = _pick(seq, 512)

    inv_freq = 1.0 / (ROPE_THETA ** (jnp.arange(0, DA_QK_DIM, 2, dtype=F32) / DA_QK_DIM))
    ang = positions.astype(F32).reshape(m, 1) * inv_freq
    cos, sin = jnp.cos(ang), jnp.sin(ang)
    cos_t = jnp.tile(cos, (1, 4))
    sin_t = jnp.tile(jnp.concatenate([-sin, sin], axis=1), (1, 2))

    lb_soft = jax.nn.softmax(hg_lower_bounds.astype(F32), axis=0)
    lb_all = jnp.cumsum(lb_soft, axis=0) - lb_soft[0:1]

    x2 = x.reshape(m, D_MODEL)
    mem2 = mem.reshape(batch * n_mem, D_MODEL)
    row = lambda v: v.reshape(1, -1).astype(F32)

    w_in_b, w_out_b, w_up_b, w_down_b = (w.astype(BF16) for w in (w_in, w_out, ffn_w_up, ffn_w_down))
    wq_b, wk_b, wv_b, wo_b = (w.astype(BF16) for w in (ca_wq, ca_wk, ca_wv, ca_wo))

    for layer in range(depth):
        hg_in, rg_in, qt, kk, vt = _inproj(x2, w_in_b, layer, cos_t, sin_t, tm_proj)

        lam_init = 0.8 - 0.6 * math.exp(-0.3 * layer)
        lp = da_lambda[layer].astype(F32)
        lam = jnp.exp(jnp.sum(lp[0] * lp[1])) - jnp.exp(jnp.sum(lp[2] * lp[3])) + lam_init
        gain = jnp.broadcast_to(da_subln_g[layer].astype(F32)[:, None], (DA_V_DIM, LANES))
        o_da = _diff_attention(jnp.full((1, LANES), lam, F32), qt, kk, vt, gain,
                               batch, seq, 1.0 - lam_init, tq, tm_proj)

        o_f, o_b = _hgrn(hg_in, row(lb_all[layer, 0]), row(lb_all[layer, 1]), batch, seq, tb_hg, 16)

        nsp = -RG_C * jax.nn.softplus(-rg_lambda[layer].astype(F32))
        h_dirs = []
        for d in range(2):
            wg = jnp.concatenate([_block_diag(rg_wa[layer, d]), _block_diag(rg_wx[layer, d])], axis=1).astype(BF16)
            bg = jnp.concatenate([rg_ba[layer, d], rg_bx[layer, d]]).reshape(1, -1).astype(F32)
            h_dirs.append(_rglru(rg_in, rg_conv_w[layer].astype(F32), row(rg_conv_b[layer]), wg, bg,
                                 row(nsp[d]), batch, seq, d == 1, tb_rg))

        kmem = _mem_proj(mem2, wk_b, layer)
        vmem = _mem_proj(mem2, wv_b, layer)
        x2 = _mix_ca(x2, o_f, o_b, hg_in, row(hg_norm_g[layer]), h_dirs[0], h_dirs[1], rg_in, o_da,
                     w_out_b, row(ln1_g[layer]), row(ln1_b[layer]), wq_b, kmem, vmem, wo_b,
                     row(ln2_g[layer]), row(ln2_b[layer]), layer, batch, seq, n_mem, alpha, tm_proj)

        x2 = _ffn(x2, w_up_b, ffn_conv_w[layer].astype(F32), row(ffn_conv_b[layer]), w_down_b,
                  row(ln3_g[layer]), row(ln3_b[layer]), layer, batch, seq, alpha, tm_ffn)

    return x2.reshape(batch, seq, D_MODEL)
```

```python
import functools
import math

import jax
import jax.numpy as jnp
from jax import lax
from jax.experimental import pallas as pl
from jax.experimental.pallas import tpu as pltpu

F32 = jnp.float32
BF16 = jnp.bfloat16

D_MODEL = 1024
N_HEADS = 4
HG_WIDTH = 256
HG_DK = 64
RG_WIDTH = 256
RG_BLOCK = 64
RG_C = 8.0
DA_WIDTH = 512
DA_V_DIM = 128
DA_QK_DIM = 64
ROPE_THETA = 10000.0
CA_HEAD_DIM = 256
D_FF = 2816
LN_EPS = 1e-5
RMS_EPS = 1e-6

LANES = 128
SUBLANES = 8
VMEM_LIMIT = 56 * 1024 * 1024

NT_DIMS = (((1,), (1,)), ((), ()))
TN_DIMS = (((0,), (0,)), ((), ()))
NEG_BIG = -1e30
TINY = 1e-30
LOG2_E = math.log2(math.e)
ATTN_UNROLL = 16
BF16_ROWS = 16
VT_ROWS = DA_V_DIM + BF16_ROWS


def _cparams(sem, flags=None):
    return pltpu.CompilerParams(dimension_semantics=sem, vmem_limit_bytes=VMEM_LIMIT, flags=flags)


def _const_spec(shape):
    nd = len(shape)
    return pl.BlockSpec(shape, lambda *_: (0,) * nd, pipeline_mode=pl.Buffered(1))


def _layer_norm(y, g, b):
    mu = jnp.mean(y, axis=-1, keepdims=True)
    d = y - mu
    var = jnp.mean(d * d, axis=-1, keepdims=True)
    return d * lax.rsqrt(var + LN_EPS) * g + b


def _gelu_tanh(x):
    return 0.5 * x * (1.0 + jnp.tanh(math.sqrt(2.0 / math.pi) * (x + 0.044715 * (x * x * x))))


def _head_block_ones(n, blk):
    r = lax.broadcasted_iota(jnp.int32, (n, n), 0) // blk
    c = lax.broadcasted_iota(jnp.int32, (n, n), 1) // blk
    return r == c


def _inproj_kernel(x_ref, w_ref, cos_ref, sin_ref, hg_ref, rg_ref, qm_ref, k_ref, v_ref):
    xb = x_ref[...].astype(BF16)
    o_rg = 5 * HG_WIDTH
    o_q = o_rg + 2 * RG_WIDTH
    o_k, o_v = o_q + DA_WIDTH, o_q + 2 * DA_WIDTH
    proj = lambda lo, hi: jnp.dot(xb, w_ref[:, lo:hi], preferred_element_type=F32)
    hg_ref[...] = proj(0, o_rg)
    rg_ref[...] = proj(o_rg, o_q)

    reps = DA_WIDTH // LANES
    c = jnp.concatenate([cos_ref[...]] * reps, axis=1)
    sgn_sin = jnp.concatenate([sin_ref[...]] * reps, axis=1)
    half = DA_QK_DIM // 2
    lane = lax.broadcasted_iota(jnp.int32, c.shape, 1)
    first_half = (lane % DA_QK_DIM) < half

    def rot(t):
        partner = jnp.where(first_half, pltpu.roll(t, DA_WIDTH - half, 1), pltpu.roll(t, half, 1))
        return t * c + partner * sgn_sin

    q = rot(proj(o_q, o_k)) * (DA_QK_DIM ** -0.5 * LOG2_E)
    qt = q.T
    sub = lax.broadcasted_iota(jnp.int32, qt.shape, 0)
    first = (sub % DA_V_DIM) < DA_QK_DIM
    qm_ref[0] = jnp.where(first, qt, 0.0).astype(BF16)
    qm_ref[1] = jnp.where(first, 0.0, qt).astype(BF16)
    k_ref[...] = rot(proj(o_k, o_v)).astype(BF16)
    vt = proj(o_v, o_v + DA_WIDTH).T.astype(BF16)
    ones = jnp.ones((VT_ROWS - DA_V_DIM, vt.shape[1]), BF16)
    for h in range(N_HEADS):
        v_ref[0, h * VT_ROWS:h * VT_ROWS + DA_V_DIM, :] = vt[h * DA_V_DIM:(h + 1) * DA_V_DIM]
        v_ref[0, h * VT_ROWS + DA_V_DIM:(h + 1) * VT_ROWS, :] = ones


def _inproj(x2, w, cos_t, sin_t, tm):
    m = x2.shape[0]
    row = lambda w: pl.BlockSpec((tm, w), lambda i: (i, 0))
    return pl.pallas_call(
        _inproj_kernel,
        grid=(m // tm,),
        in_specs=[row(D_MODEL), _const_spec(w.shape), row(LANES), row(LANES)],
        out_specs=[row(5 * HG_WIDTH), row(2 * RG_WIDTH),
                   pl.BlockSpec((2, DA_WIDTH, tm), lambda i: (0, 0, i)), row(DA_WIDTH),
                   pl.BlockSpec((1, N_HEADS * VT_ROWS, tm), lambda i: (i, 0, 0))],
        out_shape=[jax.ShapeDtypeStruct((m, 5 * HG_WIDTH), F32),
                   jax.ShapeDtypeStruct((m, 2 * RG_WIDTH), F32),
                   jax.ShapeDtypeStruct((2, DA_WIDTH, m), BF16),
                   jax.ShapeDtypeStruct((m, DA_WIDTH), BF16),
                   jax.ShapeDtypeStruct((m // tm, N_HEADS * VT_ROWS, tm), BF16)],
        compiler_params=_cparams(("parallel",)),
        name="inproj",
    )(x2, w, cos_t, sin_t)


def _attn_kernel(lam_ref, qt_ref, qn_ref, k_ref, vt_ref, g_ref, o_ref, q_sc, s_sc, acc_sc,
                 *, tq, tk, nk, unroll, out_scale):
    q_sc[0] = jnp.concatenate([qt_ref[0], qt_ref[1]], axis=1)
    q_sc[1] = jnp.concatenate([qn_ref[0], qn_ref[1]], axis=1)
    acc_sc[...] = jnp.zeros(acc_sc.shape, F32)

    def scores(idx, slot, tile):
        kc = k_ref[pl.ds(pl.multiple_of(idx * tk, tk), tk), :]
        s_sc[slot] = jnp.dot(kc, q_sc[tile], preferred_element_type=F32)

    def absorb(idx, slot, m_prev):
        s = s_sc[slot]
        m_next = jnp.maximum(m_prev, jnp.max(s, axis=0, keepdims=True))
        p = jnp.exp2(s - m_next).astype(BF16)
        pv = jnp.dot(vt_ref[idx], p, preferred_element_type=F32)
        acc_sc[...] = jnp.exp2(m_prev - m_next) * acc_sc[...] + pv
        return m_next

    @pl.when(pl.program_id(2) == 0)
    def _():
        scores(0, 0, 0)

    def body(t, m_run):
        for u in range(unroll):
            c = unroll * t + u
            wrap = jnp.asarray(c + 1 == nk).astype(jnp.int32)
            scores((c + 1) * (1 - wrap), (u + 1) % 2, wrap)
            m_run = absorb(c, u % 2, m_run)
        return m_run

    lax.fori_loop(0, nk // unroll, body, jnp.full((1, 2 * tq), -jnp.inf, F32))
    o = acc_sc[:DA_V_DIM, :] / acc_sc[DA_V_DIM:DA_V_DIM + 1, :]
    d = o[:, :tq] - lam_ref[...][:, :1] * o[:, tq:]
    ms = jnp.mean(d * d, axis=0, keepdims=True)
    gain = jnp.concatenate([g_ref[...]] * (tq // LANES), axis=1)
    o_ref[...] = (d * lax.rsqrt(ms + RMS_EPS) * gain * out_scale).astype(o_ref.dtype)


def _diff_attention(lam_row, qt, kk, vt, subln_g, batch, seq, out_scale, tq, tk):
    m = batch * seq
    nq = seq // tq
    nk = seq // tk
    unroll = math.gcd(nk, ATTN_UNROLL)
    assert unroll % 2 == 0, nk
    kern = functools.partial(_attn_kernel, tq=tq, tk=tk, nk=nk, unroll=unroll, out_scale=out_scale)
    return pl.pallas_call(
        kern,
        grid=(batch, N_HEADS, nq),
        in_specs=[pl.BlockSpec((1, LANES), lambda b, h, i: (0, 0)),
                  pl.BlockSpec((2, DA_V_DIM, tq), lambda b, h, i: (0, h, b * nq + i)),
                  pl.BlockSpec((2, DA_V_DIM, tq), lambda b, h, i: (0, h, b * nq + jnp.minimum(i + 1, nq - 1))),
                  pl.BlockSpec((seq, DA_V_DIM), lambda b, h, i: (b, h)),
                  pl.BlockSpec((nk, VT_ROWS, tk), lambda b, h, i: (b, h, 0)),
                  pl.BlockSpec((DA_V_DIM, LANES), lambda b, h, i: (0, 0))],
        out_specs=pl.BlockSpec((DA_V_DIM, tq), lambda b, h, i: (h, b * nq + i)),
        out_shape=jax.ShapeDtypeStruct((DA_WIDTH, m), BF16),
        scratch_shapes=[pltpu.VMEM((2, DA_V_DIM, 2 * tq), BF16), pltpu.VMEM((2, tk, 2 * tq), F32),
                        pltpu.VMEM((VT_ROWS, 2 * tq), F32)],
        compiler_params=_cparams(("parallel", "parallel", "arbitrary")),
        name="diff_attn",
    )(lam_row, qt, qt, kk, vt, subln_g)


def _hgrn_kernel(qf_ref, ff_ref, vf_ref, lbf_ref, qb_ref, fb_ref, vb_ref, lbb_ref, of_ref, ob_ref,
                 stf_sc, stb_sc, *, tb, chunk):
    @pl.when(pl.program_id(1) == 0)
    def _():
        stf_sc[...] = jnp.zeros(stf_sc.shape, F32)
        stb_sc[...] = jnp.zeros(stb_sc.shape, F32)

    _hgrn_direction(qf_ref, ff_ref, vf_ref, lbf_ref, of_ref, stf_sc, tb, chunk, False)
    _hgrn_direction(qb_ref, fb_ref, vb_ref, lbb_ref, ob_ref, stb_sc, tb, chunk, True)


def _hgrn_direction(q_ref, f_ref, v_ref, lb_ref, o_ref, st_sc, tb, chunk, reverse):
    lb = lb_ref[...]
    z = f_ref[...]
    qraw = q_ref[...]
    qs = qraw * jax.nn.sigmoid(qraw)
    g = jnp.log(lb + (1.0 - lb) * jax.nn.sigmoid(z))
    kfull = (1.0 - lb) * jax.nn.sigmoid(-z)
    vfull = v_ref[...]

    r = lax.broadcasted_iota(jnp.int32, (tb, tb), 0)
    c = lax.broadcasted_iota(jnp.int32, (tb, tb), 1)
    tri = ((r // chunk) == (c // chunk)) & ((c >= r) if reverse else (c <= r))
    g1 = g.astype(BF16)
    r1 = g - g1.astype(F32)
    g2 = r1.astype(BF16)
    g3 = (r1 - g2.astype(F32)).astype(BF16)
    tri16 = tri.astype(BF16)
    b = (jnp.dot(tri16, g1, preferred_element_type=F32) + jnp.dot(tri16, g2, preferred_element_type=F32)
         + jnp.dot(tri16, g3, preferred_element_type=F32))

    bones = _head_block_ones(HG_WIDTH, HG_DK).astype(BF16)
    pair_head = _head_block_ones(LANES, HG_DK)
    half = chunk // 2
    rows = lax.broadcasted_iota(jnp.int32, (half, HG_WIDTH), 0)
    zeros_half = jnp.zeros((half, HG_WIDTH), F32)

    order = range(tb // chunk - 1, -1, -1) if reverse else range(tb // chunk)
    for ci in order:
        sl = slice(ci * chunk, (ci + 1) * chunk)
        bc, qc, kc, vc = b[sl], qs[sl], kfull[sl], vfull[sl]
        b_end = bc[0:1] if reverse else bc[chunk - 1:chunk]
        qt = (qc * jnp.exp(bc)).astype(BF16)
        kh = (kc * jnp.exp(b_end - bc)).astype(BF16)
        vb = vc.astype(BF16)
        dec = jnp.exp(b_end)
        inter = []
        for p in range(HG_WIDTH // LANES):
            ln = slice(p * LANES, (p + 1) * LANES)
            st = st_sc[p]
            inter.append(lax.dot_general(qt[:, ln], st.astype(BF16), NT_DIMS, preferred_element_type=F32))
            kv = lax.dot_general(vb[:, ln], kh[:, ln], TN_DIMS, preferred_element_type=F32)
            st_sc[p] = jnp.where(pair_head, st * dec[:, ln] + kv, 0.0)
        o_c = jnp.concatenate(inter, axis=1)
        pieces = []
        for s in range(chunk):
            parts = []
            for hf in range(2):
                lo = hf * half
                live = (lo + half - 1 >= s) if not reverse else (lo <= s)
                if not live:
                    parts.append(zeros_half)
                    continue
                diff = bc[lo:lo + half] - bc[s:s + 1]
                all_kept = (lo + half - 1 <= s) if reverse else (lo >= s)
                if not all_kept:
                    keep = (rows + lo <= s) if reverse else (rows + lo >= s)
                    diff = jnp.where(keep, diff, NEG_BIG)
                e = jnp.exp(diff)
                parts.append(qc[lo:lo + half] * e * kc[s:s + 1])
            pieces.append(jnp.concatenate(parts, axis=0).astype(BF16))
        att = jnp.dot(jnp.concatenate(pieces, axis=0), bones, preferred_element_type=F32)
        for s in range(chunk):
            o_c = o_c + att[s * chunk:(s + 1) * chunk] * vc[s:s + 1]
        o_ref[sl, :] = o_c


def _hgrn(hg_in, lb_fwd, lb_bwd, batch, seq, tb, chunk):
    m = batch * seq
    nb = seq // tb
    fwd = lambda cidx: pl.BlockSpec((tb, HG_WIDTH), lambda b, j: (b * nb + j, cidx))
    bwd = lambda cidx: pl.BlockSpec((tb, HG_WIDTH), lambda b, j: (b * nb + nb - 1 - j, cidx))
    vec = pl.BlockSpec((1, HG_WIDTH), lambda b, j: (0, 0))
    state = pltpu.VMEM((HG_WIDTH // LANES, LANES, LANES), F32)
    return pl.pallas_call(
        functools.partial(_hgrn_kernel, tb=tb, chunk=chunk),
        grid=(batch, nb),
        in_specs=[fwd(0), fwd(1), fwd(3), vec, bwd(0), bwd(2), bwd(3), vec],
        out_specs=[fwd(0), bwd(0)],
        out_shape=[jax.ShapeDtypeStruct((m, HG_WIDTH), F32)] * 2,
        scratch_shapes=[state, state],
        compiler_params=_cparams(("parallel", "arbitrary")),
        name="hgrn",
    )(hg_in, hg_in, hg_in, lb_fwd, hg_in, hg_in, hg_in, lb_bwd)


def _rglru_kernel(xp_ref, x_ref, xn_ref, cw_ref, cb_ref, wg_ref, bg_ref, nsp_ref, h_ref, carry_sc,
                  *, tb, nb, reverse):
    j = pl.program_id(1)
    pos = (nb - 1 - j) if reverse else j

    @pl.when(j == 0)
    def _():
        carry_sc[...] = jnp.zeros(carry_sc.shape, F32)

    xp = jnp.where(pos > 0, xp_ref[...], 0.0)
    xn = jnp.where(pos < nb - 1, xn_ref[...], 0.0)
    xe = jnp.concatenate([xp, x_ref[...], xn], axis=0)
    cw = cw_ref[...]
    ne = tb + 2 * SUBLANES
    mid = slice(SUBLANES, SUBLANES + tb)
    tap = lambda k: pltpu.roll(xe, (-k) % ne, 0)[mid]
    xc = cw[0:1] * tap(-2) + cw[1:2] * tap(-1) + cw[2:3] * xe[mid] + cw[3:4] * tap(1) + cb_ref[...]
    gates = jnp.dot(xc.astype(BF16), wg_ref[...], preferred_element_type=F32) + bg_ref[...]
    rgate = jax.nn.sigmoid(gates[:, :RG_WIDTH])
    igate = jax.nn.sigmoid(gates[:, RG_WIDTH:])
    log_a = nsp_ref[...] * rgate
    a = jnp.exp(log_a)
    one_m_a2 = 1.0 - a * a
    u = one_m_a2 * lax.rsqrt(jnp.maximum(one_m_a2, TINY)) * (igate * xc)

    rows = lax.broadcasted_iota(jnp.int32, (tb, RG_WIDTH), 0) % SUBLANES
    d = 1
    while d < SUBLANES:
        if reverse:
            a_sh, u_sh, ok = pltpu.roll(a, tb - d, 0), pltpu.roll(u, tb - d, 0), rows < SUBLANES - d
        else:
            a_sh, u_sh, ok = pltpu.roll(a, d, 0), pltpu.roll(u, d, 0), rows >= d
        u = jnp.where(ok, a * u_sh + u, u)
        a = jnp.where(ok, a * a_sh, a)
        d *= 2
    ngroups = tb // SUBLANES
    carry = carry_sc[...]
    out = [None] * ngroups
    for gi in (range(ngroups - 1, -1, -1) if reverse else range(ngroups)):
        sl = slice(gi * SUBLANES, (gi + 1) * SUBLANES)
        hg = a[sl] * carry + u[sl]
        out[gi] = hg
        carry = hg[0:1] if reverse else hg[SUBLANES - 1:SUBLANES]
    h_ref[...] = jnp.concatenate(out, axis=0)
    carry_sc[...] = carry


def _rglru(rg_in, conv_w, conv_b, wg, bg, nsp, batch, seq, reverse, tb):
    m = batch * seq
    nb = seq // tb
    per8 = tb // SUBLANES
    last8 = m // SUBLANES - 1
    pos = (lambda j: nb - 1 - j) if reverse else (lambda j: j)
    blk = lambda b, j: b * nb + pos(j)
    kern = functools.partial(_rglru_kernel, tb=tb, nb=nb, reverse=reverse)
    return pl.pallas_call(
        kern,
        grid=(batch, nb),
        in_specs=[pl.BlockSpec((SUBLANES, RG_WIDTH), lambda b, j: (jnp.maximum(blk(b, j) * per8 - 1, 0), 0)),
                  pl.BlockSpec((tb, RG_WIDTH), lambda b, j: (blk(b, j), 0)),
                  pl.BlockSpec((SUBLANES, RG_WIDTH), lambda b, j: (jnp.minimum((blk(b, j) + 1) * per8, last8), 0)),
                  pl.BlockSpec(conv_w.shape, lambda b, j: (0, 0)),
                  pl.BlockSpec(conv_b.shape, lambda b, j: (0, 0)),
                  pl.BlockSpec(wg.shape, lambda b, j: (0, 0)),
                  pl.BlockSpec(bg.shape, lambda b, j: (0, 0)),
                  pl.BlockSpec(nsp.shape, lambda b, j: (0, 0))],
        out_specs=pl.BlockSpec((tb, RG_WIDTH), lambda b, j: (blk(b, j), 0)),
        out_shape=jax.ShapeDtypeStruct((m, RG_WIDTH), F32),
        scratch_shapes=[pltpu.VMEM((1, RG_WIDTH), F32)],
        compiler_params=_cparams(("parallel", "arbitrary")),
        name="rglru_bwd" if reverse else "rglru_fwd",
    )(rg_in, rg_in, rg_in, conv_w, conv_b, wg, bg, nsp)


def _mix_ca_kernel(x_ref, of_ref, ob_ref, hgg_ref, ng_ref, hf_ref, hb_ref, ry_ref, da_ref, wout_ref,
                   l1g_ref, l1b_ref, wq_ref, k_ref, v_ref, wo_ref, l2g_ref, l2b_ref, out_ref, *, alpha):
    o = of_ref[...] + ob_ref[...]
    bones = _head_block_ones(HG_WIDTH, HG_DK).astype(BF16)
    sq = o * o
    hi = sq.astype(BF16)
    lo = (sq - hi.astype(F32)).astype(BF16)
    ms = (jnp.dot(hi, bones, preferred_element_type=F32)
          + jnp.dot(lo, bones, preferred_element_type=F32)) * (1.0 / HG_DK)
    gate = hgg_ref[...]
    o_hg = o * lax.rsqrt(ms + RMS_EPS) * ng_ref[...] * (gate * jax.nn.sigmoid(gate))
    o_rg = (hf_ref[...] + hb_ref[...]) * _gelu_tanh(ry_ref[...])
    cat = jnp.concatenate([o_hg.astype(BF16), o_rg.astype(BF16)], axis=1)
    n_rec = HG_WIDTH + RG_WIDTH
    mix = (jnp.dot(cat, wout_ref[:n_rec, :], preferred_element_type=F32)
           + lax.dot_general(da_ref[...], wout_ref[n_rec:, :], TN_DIMS, preferred_element_type=F32))
    x = _layer_norm(alpha * x_ref[...] + mix, l1g_ref[...], l1b_ref[...])

    q = jnp.dot(x.astype(BF16), wq_ref[...], preferred_element_type=F32) * (CA_HEAD_DIM ** -0.5)
    outs = []
    for h in range(N_HEADS):
        sl = slice(h * CA_HEAD_DIM, (h + 1) * CA_HEAD_DIM)
        s = lax.dot_general(q[:, sl].astype(BF16), k_ref[:, sl], NT_DIMS, preferred_element_type=F32)
        p = jnp.exp(s - jnp.max(s, axis=1, keepdims=True))
        l = jnp.sum(p, axis=1, keepdims=True)
        o = jnp.dot(p.astype(BF16), v_ref[:, sl], preferred_element_type=F32) / l
        outs.append(o.astype(BF16))
    ca = jnp.dot(jnp.concatenate(outs, axis=1), wo_ref[...], preferred_element_type=F32)
    out_ref[...] = _layer_norm(alpha * x + ca, l2g_ref[...], l2b_ref[...])


def _mix_ca(x2, o_f, o_b, hg_in, norm_g, h_f, h_b, rg_in, o_da, w_out, ln1_g, ln1_b,
            wq, kmem, vmem, wo, ln2_g, ln2_b, batch, seq, n_mem, alpha, tm):
    m = x2.shape[0]
    nt = seq // tm
    row = lambda w, cidx=0: pl.BlockSpec((tm, w), lambda b, i: (b * nt + i, cidx))
    vec = lambda w: pl.BlockSpec((1, w), lambda b, i: (0, 0))
    mem_blk = pl.BlockSpec((n_mem, D_MODEL), lambda b, i: (b, 0))
    return pl.pallas_call(
        functools.partial(_mix_ca_kernel, alpha=alpha),
        grid=(batch, nt),
        in_specs=[row(D_MODEL), row(HG_WIDTH), row(HG_WIDTH), row(HG_WIDTH, 4), vec(HG_WIDTH),
                  row(RG_WIDTH), row(RG_WIDTH), row(RG_WIDTH, 1),
                  pl.BlockSpec((DA_WIDTH, tm), lambda b, i: (0, b * nt + i)),
                  _const_spec(w_out.shape), vec(D_MODEL), vec(D_MODEL),
                  _const_spec(wq.shape), mem_blk, mem_blk, _const_spec(wo.shape), vec(D_MODEL), vec(D_MODEL)],
        out_specs=row(D_MODEL),
        out_shape=jax.ShapeDtypeStruct((m, D_MODEL), F32),
        compiler_params=_cparams(("parallel", "parallel")),
        name="mix_cross_attn",
    )(x2, o_f, o_b, hg_in, norm_g, h_f, h_b, rg_in, o_da, w_out, ln1_g, ln1_b,
      wq, kmem, vmem, wo, ln2_g, ln2_b)


def _matmul_kernel(a_ref, w_ref, o_ref):
    o_ref[...] = jnp.dot(a_ref[...].astype(BF16), w_ref[...], preferred_element_type=F32).astype(o_ref.dtype)


def _mem_proj(mem2, w):
    m, n = mem2.shape[0], w.shape[1]
    return pl.pallas_call(
        _matmul_kernel,
        grid=(1,),
        in_specs=[pl.BlockSpec(mem2.shape, lambda i: (0, 0)), pl.BlockSpec(w.shape, lambda i: (0, 0))],
        out_specs=pl.BlockSpec((m, n), lambda i: (0, 0)),
        out_shape=jax.ShapeDtypeStruct((m, n), BF16),
        compiler_params=_cparams(("arbitrary",)),
        name="mem_proj",
    )(mem2, w)


def _ffn_kernel(xp_ref, x_ref, xn_ref, wu_ref, cw_ref, cb_ref, wd_ref, lg_ref, lb_ref, out_ref,
                *, tm, nt, alpha):
    pos = pl.program_id(1)
    x = x_ref[...]
    xp = jnp.where(pos > 0, xp_ref[...], 0.0)
    xn = jnp.where(pos < nt - 1, xn_ref[...], 0.0)
    xe = jnp.concatenate([xp, x, xn], axis=0).astype(BF16)
    gate = jnp.dot(xe, wu_ref[:, :D_FF], preferred_element_type=F32)
    cw = cw_ref[...]
    ne = tm + 2 * SUBLANES
    mid = slice(SUBLANES, SUBLANES + tm)
    gc = (cw[0:1] * pltpu.roll(gate, 1, 0)[mid] + cw[1:2] * gate[mid]
          + cw[2:3] * pltpu.roll(gate, ne - 1, 0)[mid] + cb_ref[...])
    val = jnp.dot(x.astype(BF16), wu_ref[:, D_FF:], preferred_element_type=F32)
    hid = (_gelu_tanh(gc) * val).astype(BF16)
    ff = jnp.dot(hid, wd_ref[...], preferred_element_type=F32)
    out_ref[...] = _layer_norm(alpha * x + ff, lg_ref[...], lb_ref[...])


def _ffn(x2, wu, conv_w, conv_b, wd, ln_g, ln_b, batch, seq, alpha, tm):
    m = x2.shape[0]
    nt = seq // tm
    per8 = tm // SUBLANES
    last8 = m // SUBLANES - 1
    blk = lambda b, i: b * nt + i
    vec = pl.BlockSpec((1, D_MODEL), lambda b, i: (0, 0))
    return pl.pallas_call(
        functools.partial(_ffn_kernel, tm=tm, nt=nt, alpha=alpha),
        grid=(batch, nt),
        in_specs=[pl.BlockSpec((SUBLANES, D_MODEL), lambda b, i: (jnp.maximum(blk(b, i) * per8 - 1, 0), 0)),
                  pl.BlockSpec((tm, D_MODEL), lambda b, i: (blk(b, i), 0)),
                  pl.BlockSpec((SUBLANES, D_MODEL), lambda b, i: (jnp.minimum((blk(b, i) + 1) * per8, last8), 0)),
                  _const_spec(wu.shape),
                  pl.BlockSpec(conv_w.shape, lambda b, i: (0, 0)),
                  pl.BlockSpec(conv_b.shape, lambda b, i: (0, 0)),
                  _const_spec(wd.shape), vec, vec],
        out_specs=pl.BlockSpec((tm, D_MODEL), lambda b, i: (blk(b, i), 0)),
        out_shape=jax.ShapeDtypeStruct((m, D_MODEL), F32),
        compiler_params=_cparams(("parallel", "parallel")),
        name="conv_glu",
    )(x2, x2, x2, wu, conv_w, conv_b, wd, ln_g, ln_b)


def _block_diag(w):
    n, blk, _ = w.shape
    eye = jnp.eye(n, dtype=w.dtype)
    return jnp.einsum("nij,nm->nimj", w, eye).reshape(n * blk, n * blk)


def _pick(seq, pref):
    t = min(seq, pref)
    assert seq % t == 0, (seq, pref)
    return t


def kernel(x, mem, positions, w_in, hg_lower_bounds, hg_norm_g, rg_conv_w, rg_conv_b, rg_wa, rg_ba, rg_wx, rg_bx, rg_lambda, da_lambda, da_subln_g, w_out, ln1_g, ln1_b, ca_wq, ca_wk, ca_wv, ca_wo, ln2_g, ln2_b, ffn_w_up, ffn_conv_w, ffn_conv_b, ffn_w_down, ln3_g, ln3_b):
    batch, seq, _ = x.shape
    n_mem = mem.shape[1]
    depth = w_in.shape[0]
    alpha = (2 * depth) ** 0.25
    m = batch * seq

    tm_proj = _pick(seq, 512)
    tm_ffn = _pick(seq, 256)
    tq = _pick(seq, 256)
    tb_hg = _pick(seq, 128)
    tb_rg = _pick(seq, 512)

    inv_freq = 1.0 / (ROPE_THETA ** (jnp.arange(0, DA_QK_DIM, 2, dtype=F32) / DA_QK_DIM))
    ang = positions.astype(F32).reshape(m, 1) * inv_freq
    cos, sin = jnp.cos(ang), jnp.sin(ang)
    cos_t = jnp.tile(cos, (1, 4))
    sin_t = jnp.tile(jnp.concatenate([-sin, sin], axis=1), (1, 2))

    lb_soft = jax.nn.softmax(hg_lower_bounds.astype(F32), axis=0)
    lb_all = jnp.cumsum(lb_soft, axis=0) - lb_soft[0:1]

    x2 = x.reshape(m, D_MODEL)
    mem2 = mem.reshape(batch * n_mem, D_MODEL)
    row = lambda v: v.reshape(1, -1).astype(F32)

    for layer in range(depth):
        hg_in, rg_in, qt, kk, vt = _inproj(x2, w_in[layer].astype(BF16), cos_t, sin_t, tm_proj)

        lam_init = 0.8 - 0.6 * math.exp(-0.3 * layer)
        lp = da_lambda[layer].astype(F32)
        lam = jnp.exp(jnp.sum(lp[0] * lp[1])) - jnp.exp(jnp.sum(lp[2] * lp[3])) + lam_init
        gain = jnp.broadcast_to(da_subln_g[layer].astype(F32)[:, None], (DA_V_DIM, LANES))
        o_da = _diff_attention(jnp.full((1, LANES), lam, F32), qt, kk, vt, gain,
                               batch, seq, 1.0 - lam_init, tq, tm_proj)

        o_f, o_b = _hgrn(hg_in, row(lb_all[layer, 0]), row(lb_all[layer, 1]), batch, seq, tb_hg, 16)

        nsp = -RG_C * jax.nn.softplus(-rg_lambda[layer].astype(F32))
        h_dirs = []
        for d in range(2):
            wg = jnp.concatenate([_block_diag(rg_wa[layer, d]), _block_diag(rg_wx[layer, d])], axis=1).astype(BF16)
            bg = jnp.concatenate([rg_ba[layer, d], rg_bx[layer, d]]).reshape(1, -1).astype(F32)
            h_dirs.append(_rglru(rg_in, rg_conv_w[layer].astype(F32), row(rg_conv_b[layer]), wg, bg,
                                 row(nsp[d]), batch, seq, d == 1, tb_rg))

        kmem = _mem_proj(mem2, ca_wk[layer].astype(BF16))
        vmem = _mem_proj(mem2, ca_wv[layer].astype(BF16))
        x2 = _mix_ca(x2, o_f, o_b, hg_in, row(hg_norm_g[layer]), h_dirs[0], h_dirs[1], rg_in, o_da,
                     w_out[layer].astype(BF16), row(ln1_g[layer]), row(ln1_b[layer]),
                     ca_wq[layer].astype(BF16), kmem, vmem, ca_wo[layer].astype(BF16),
                     row(ln2_g[layer]), row(ln2_b[layer]), batch, seq, n_mem, alpha, tm_proj)

        x2 = _ffn(x2, ffn_w_up[layer].astype(BF16), ffn_conv_w[layer].astype(F32), row(ffn_conv_b[layer]),
                  ffn_w_down[layer].astype(BF16), row(ln3_g[layer]), row(ln3_b[layer]), batch, seq, alpha, tm_ffn)

    return x2.reshape(batch, seq, D_MODEL)
```
